```python
import math
import jax, jax.numpy as jnp
from jax import lax
import numpy as np

D_MODEL = 2048
BATCH = 4
SEQ = 4096
DEPTH = 2

N_ATT_HEADS = 8
ATT_HALF_DIM = 64
ATT_V_DIM = 2 * ATT_HALF_DIM
D_ATT = N_ATT_HEADS * ATT_V_DIM
Q_BLOCK = 128
D_CONV = D_MODEL // 2
CONV_WIDTH = 31
D_IN = 3 * D_ATT + 2 * D_CONV + 2 * D_MODEL
N_EXPERTS = 32
TOP_K = 4
D_FF = D_MODEL
SWIGLU_LIMIT = 7.0
SWIGLU_ALPHA = 1.702
MOE_BLOCK = 128
LN_EPS = 1e-5
DEEPNORM_ALPHA = (2 * DEPTH) ** 0.25
DEEPNORM_BETA = (8 * DEPTH) ** -0.25

kernel_name = 'hybrid_diffattn_conformer_moe_deepnorm'


def layer_norm(x, g, b):
    xf = x.astype(jnp.float32)
    mu = jnp.mean(xf, axis=-1, keepdims=True)
    var = jnp.mean(jnp.square(xf - mu), axis=-1, keepdims=True)
    return ((xf - mu) * lax.rsqrt(var + LN_EPS)).astype(x.dtype) * g + b


def rms_norm(x, g):
    xf = x.astype(jnp.float32)
    return (xf * lax.rsqrt(jnp.mean(xf * xf, axis=-1, keepdims=True) + LN_EPS)).astype(x.dtype) * g


def alibi_slopes(n_heads):
    return jnp.asarray(2.0 ** (-8.0 * np.arange(1, n_heads + 1) / n_heads), dtype=jnp.float32)


def diff_attention(q, k, v, lam, slopes):
    B, S = q.shape[0], q.shape[1]
    n_blocks = S // Q_BLOCK
    scale = ATT_HALF_DIM ** -0.5
    q_blocks = q.reshape(B, n_blocks, Q_BLOCK, N_ATT_HEADS, 2, ATT_HALF_DIM).transpose(1, 0, 2, 3, 4, 5)
    pos_k = jnp.arange(S)

    def one_block(args):
        q_blk, start = args
        s = jnp.einsum('bqhmd,bkhmd->bhmqk', q_blk, k,
                       preferred_element_type=jnp.float32) * scale
        pos_q = start + jnp.arange(Q_BLOCK)
        dist = jnp.abs(pos_q[:, None] - pos_k[None, :]).astype(jnp.float32)
        s = s - slopes[None, :, None, None, None] * dist
        p = jax.nn.softmax(s, axis=-1)
        a = p[:, :, 0] - lam * p[:, :, 1]
        return jnp.einsum('bhqk,bkhe->bqhe', a.astype(v.dtype), v)

    starts = jnp.arange(n_blocks) * Q_BLOCK
    o = lax.map(one_block, (q_blocks, starts))
    return o.transpose(1, 0, 2, 3, 4).reshape(B, S, N_ATT_HEADS, ATT_V_DIM)


def conformer_conv(u, w_dw, b_dw, g_ln, b_ln):
    val, gate = jnp.split(u, 2, axis=-1)
    h = val * jax.nn.sigmoid(gate)
    pad = CONV_WIDTH // 2
    h = lax.conv_general_dilated(h, w_dw, window_strides=(1,), padding=[(pad, pad)],
                                 dimension_numbers=('NWC', 'WIO', 'NWC'),
                                 feature_group_count=D_CONV) + b_dw
    h = layer_norm(h, g_ln, b_ln)
    return jax.nn.silu(h)


def mixer(h, lam_init, w_in, b_gate, lq1, lk1, lq2, lk2, subln_g, w_dw, b_dw,
          cln_g, cln_b, w_pa, w_pc, b_pc, w_out, b_out, slopes):
    B, S, _ = h.shape
    proj = jnp.einsum('bsd,de->bse', h, w_in)
    splits = [D_ATT, 2 * D_ATT, 3 * D_ATT, 3 * D_ATT + 2 * D_CONV]
    q, k, v, u, g = jnp.split(proj, splits, axis=-1)
    q = q.reshape(B, S, N_ATT_HEADS, 2, ATT_HALF_DIM)
    k = k.reshape(B, S, N_ATT_HEADS, 2, ATT_HALF_DIM)
    v = v.reshape(B, S, N_ATT_HEADS, ATT_V_DIM)
    lam = (jnp.exp(jnp.sum(lq1.astype(jnp.float32) * lk1.astype(jnp.float32)))
           - jnp.exp(jnp.sum(lq2.astype(jnp.float32) * lk2.astype(jnp.float32))) + lam_init)
    o = diff_attention(q, k, v, lam, slopes)
    o = rms_norm(o, subln_g) * (1.0 - lam_init)
    y_att = jnp.einsum('bsa,ad->bsd', o.reshape(B, S, D_ATT), w_pa)
    c = conformer_conv(u, w_dw, b_dw, cln_g, cln_b)
    y_conv = jnp.einsum('bsc,cd->bsd', c, w_pc) + b_pc
    gates = jax.nn.sigmoid(g + b_gate)
    g_att, g_conv = jnp.split(gates, 2, axis=-1)
    merged = g_att * y_att + g_conv * y_conv
    return jnp.einsum('bsd,de->bse', merged, w_out) + b_out


def moe(x2, w_router, b_router, w_gu, b_gu, w_dn, b_dn):
    N = x2.shape[0]
    logits = (x2 @ w_router + b_router).astype(jnp.float32)
    top_val, top_idx = lax.top_k(logits, TOP_K)
    gate = jax.nn.softmax(top_val, axis=-1)
    e_flat = top_idx.reshape(-1)
    tok_flat = jnp.repeat(jnp.arange(N, dtype=jnp.int32), TOP_K)
    w_flat = gate.reshape(-1)
    order = jnp.argsort(e_flat)
    e_s, tok_s, w_s = e_flat[order], tok_flat[order], w_flat[order]
    counts = jnp.bincount(e_flat, length=N_EXPERTS)
    padded = ((counts + MOE_BLOCK - 1) // MOE_BLOCK) * MOE_BLOCK
    start = jnp.cumsum(counts) - counts
    pend = jnp.cumsum(padded)
    pstart = pend - padded
    rows = pstart[e_s] + (jnp.arange(N * TOP_K) - start[e_s])
    n_rows = N * TOP_K + N_EXPERTS * MOE_BLOCK
    n_blocks = n_rows // MOE_BLOCK
    row_tok = jnp.zeros((n_rows,), jnp.int32).at[rows].set(tok_s)
    row_w = jnp.zeros((n_rows,), jnp.float32).at[rows].set(w_s)
    block_e = jnp.minimum(jnp.searchsorted(pend, jnp.arange(n_blocks) * MOE_BLOCK, side='right'),
                          N_EXPERTS - 1)

    def one_block(args):
        tok, wt, e = args
        xb = x2[tok]
        gu = xb @ w_gu[e] + b_gu[e]
        g, u = jnp.split(gu, 2, axis=-1)
        g = jnp.minimum(g, SWIGLU_LIMIT)
        u = jnp.clip(u, -SWIGLU_LIMIT, SWIGLU_LIMIT)
        hid = g * jax.nn.sigmoid(SWIGLU_ALPHA * g) * (u + 1.0)
        y = hid @ w_dn[e] + b_dn[e]
        return y * wt[:, None].astype(y.dtype)

    y_rows = lax.map(one_block, (row_tok.reshape(n_blocks, MOE_BLOCK),
                                 row_w.reshape(n_blocks, MOE_BLOCK), block_e))
    return jnp.zeros_like(x2).at[row_tok].add(y_rows.reshape(n_rows, x2.shape[1]))


def setup_inputs(seed: int = 0) -> dict:
    key = jax.random.key(seed)
    keys = list(jax.random.split(key, 40))
    L, D, E, F = DEPTH, D_MODEL, N_EXPERTS, D_FF

    def nrm(i, shape, scale):
        return jax.random.normal(keys[i], shape, jnp.float32) * scale

    x = nrm(0, (BATCH, SEQ, D), 1.0)
    ln_in_g = 1.0 + nrm(1, (D,), 0.02)
    ln_in_b = nrm(2, (D,), 0.02)
    w_qk = nrm(3, (L, D, 2 * D_ATT), D ** -0.5)
    w_v = nrm(4, (L, D, D_ATT), DEEPNORM_BETA * D ** -0.5)
    w_rest = nrm(5, (L, D, 2 * D_CONV + 2 * D), D ** -0.5)
    w_in = jnp.concatenate([w_qk, w_v, w_rest], axis=-1)
    b_gate = nrm(6, (L, 2 * D), 0.02)
    lambda_q1 = nrm(7, (L, ATT_HALF_DIM), 0.1)
    lambda_k1 = nrm(8, (L, ATT_HALF_DIM), 0.1)
    lambda_q2 = nrm(9, (L, ATT_HALF_DIM), 0.1)
    lambda_k2 = nrm(10, (L, ATT_HALF_DIM), 0.1)
    subln_g = 1.0 + nrm(11, (L, ATT_V_DIM), 0.02)
    w_dw = nrm(12, (L, CONV_WIDTH, 1, D_CONV), CONV_WIDTH ** -0.5)
    b_dw = nrm(13, (L, D_CONV), 0.02)
    conv_ln_g = 1.0 + nrm(14, (L, D_CONV), 0.02)
    conv_ln_b = nrm(15, (L, D_CONV), 0.02)
    w_pa = nrm(16, (L, D_ATT, D), DEEPNORM_BETA * D_ATT ** -0.5)
    w_pc = nrm(17, (L, D_CONV, D), DEEPNORM_BETA * D_CONV ** -0.5)
    b_pc = nrm(18, (L, D), 0.02)
    w_out = nrm(19, (L, D, D), DEEPNORM_BETA * D ** -0.5)
    b_out = nrm(20, (L, D), 0.02)
    ln1_g = 1.0 + nrm(21, (L, D), 0.02)
    ln1_b = nrm(22, (L, D), 0.02)
    w_router = nrm(23, (L, D, E), D ** -0.5)
    b_router = nrm(24, (L, E), 0.01)
    w_gu = nrm(25, (L, E, D, 2 * F), D ** -0.5)
    b_gu = nrm(26, (L, E, 2 * F), 0.02)
    w_dn = nrm(27, (L, E, F, D), DEEPNORM_BETA * F ** -0.5)
    b_dn = nrm(28, (L, E, D), 0.02)
    ln2_g = 1.0 + nrm(29, (L, D), 0.02)
    ln2_b = nrm(30, (L, D), 0.02)
    return {'x': x, 'ln_in_g': ln_in_g, 'ln_in_b': ln_in_b, 'w_in': w_in, 'b_gate': b_gate,
            'lambda_q1': lambda_q1, 'lambda_k1': lambda_k1, 'lambda_q2': lambda_q2,
            'lambda_k2': lambda_k2, 'subln_g': subln_g, 'w_dw': w_dw, 'b_dw': b_dw,
            'conv_ln_g': conv_ln_g, 'conv_ln_b': conv_ln_b, 'w_pa': w_pa, 'w_pc': w_pc,
            'b_pc': b_pc, 'w_out': w_out, 'b_out': b_out, 'ln1_g': ln1_g, 'ln1_b': ln1_b,
            'w_router': w_router, 'b_router': b_router, 'w_gu': w_gu, 'b_gu': b_gu,
            'w_dn': w_dn, 'b_dn': b_dn, 'ln2_g': ln2_g, 'ln2_b': ln2_b}


def reference(x, ln_in_g, ln_in_b, w_in, b_gate, lambda_q1, lambda_k1, lambda_q2, lambda_k2,
              subln_g, w_dw, b_dw, conv_ln_g, conv_ln_b, w_pa, w_pc, b_pc, w_out, b_out,
              ln1_g, ln1_b, w_router, b_router, w_gu, b_gu, w_dn, b_dn, ln2_g, ln2_b):
    B, S, D = x.shape
    slopes = alibi_slopes(N_ATT_HEADS)
    h = layer_norm(x, ln_in_g, ln_in_b)
    for l in range(DEPTH):
        lam_init = 0.8 - 0.6 * math.exp(-0.3 * l)
        mix = mixer(h, lam_init, w_in[l], b_gate[l], lambda_q1[l], lambda_k1[l],
                    lambda_q2[l], lambda_k2[l], subln_g[l], w_dw[l], b_dw[l],
                    conv_ln_g[l], conv_ln_b[l], w_pa[l], w_pc[l], b_pc[l],
                    w_out[l], b_out[l], slopes)
        h = layer_norm(DEEPNORM_ALPHA * h + mix, ln1_g[l], ln1_b[l])
        ff = moe(h.reshape(B * S, D), w_router[l], b_router[l], w_gu[l], b_gu[l],
                 w_dn[l], b_dn[l]).reshape(B, S, D)
        h = layer_norm(DEEPNORM_ALPHA * h + ff, ln2_g[l], ln2_b[l])
    return h
```

```python
import functools
import math

import numpy as np
import jax
import jax.numpy as jnp
from jax import lax
from jax.experimental import pallas as pl
from jax.experimental.pallas import tpu as pltpu

F32 = jnp.float32
BF16 = jnp.bfloat16
I32 = jnp.int32

LN_EPS = 1e-5
ATT_HALF_DIM = 64
ATT_V_DIM = 2 * ATT_HALF_DIM
TOP_K = 4
SWIGLU_LIMIT = 7.0
SWIGLU_ALPHA = 1.702
LOG2E = 1.4426950408889634
CONV_HALO = 16
MIB = 1024 * 1024
NEG_BIG = -1e30


def _cparams(semantics, vmem_mib):
    return pltpu.CompilerParams(dimension_semantics=semantics, vmem_limit_bytes=vmem_mib * MIB)


def _tile(total, target, quantum=128):
    if total <= target:
        return total
    t = (target // quantum) * quantum
    while total % t:
        t -= quantum
    return t


def _ln_rows(x, g, b):
    mu = jnp.mean(x, axis=-1, keepdims=True)
    xc = x - mu
    var = jnp.mean(xc * xc, axis=-1, keepdims=True)
    return xc * lax.rsqrt(var + LN_EPS) * g + b


def _ln_kernel(x_ref, g_ref, b_ref, o_ref, ob_ref):
    y = _ln_rows(x_ref[...], g_ref[...], b_ref[...])
    o_ref[...] = y
    ob_ref[...] = y.astype(BF16)


def _layer_norm_in(x2, g, b, tm):
    n, d = x2.shape
    row = pl.BlockSpec((tm, d), lambda i: (i, 0))
    vec = pl.BlockSpec((1, d), lambda i: (0, 0))
    return pl.pallas_call(
        _ln_kernel, grid=(n // tm,), in_specs=[row, vec, vec], out_specs=[row, row],
        out_shape=[jax.ShapeDtypeStruct((n, d), F32), jax.ShapeDtypeStruct((n, d), BF16)],
        compiler_params=_cparams(("parallel",), 48), name="ln_in",
    )(x2, g.reshape(1, d), b.reshape(1, d))


def _mm_kernel(x_ref, w_ref, o_ref):
    o_ref[...] = jnp.dot(x_ref[...], w_ref[...], preferred_element_type=F32).astype(o_ref.dtype)


def _matmul(x, w, tm, tn, out_dtype, name):
    n, kd = x.shape
    m = w.shape[1]
    return pl.pallas_call(
        _mm_kernel, grid=(n // tm, m // tn),
        in_specs=[pl.BlockSpec((tm, kd), lambda i, j: (i, 0)), pl.BlockSpec((kd, tn), lambda i, j: (0, j))],
        out_specs=pl.BlockSpec((tm, tn), lambda i, j: (i, j)),
        out_shape=jax.ShapeDtypeStruct((n, m), out_dtype),
        compiler_params=_cparams(("parallel", "parallel"), 48), name=name,
    )(x, w)


def _attn_kernel(slopes_ref, q_ref, k_ref, v_ref, lamv_ref, g_ref, o_ref,
                 vt_ref, qz_ref, r_ref, m_ref, l_ref, acc_ref, *, t, nk, lam_init, scale):
    h = pl.program_id(1)
    iq = pl.program_id(2)
    slope2 = slopes_ref[h] * LOG2E

    @pl.when(iq == 0)
    def _():
        for j in range(nk):
            vt_ref[j] = v_ref[0, j * t:(j + 1) * t, :].astype(F32).T.astype(BF16)
        qi = lax.broadcasted_iota(I32, (t, t), 1)
        kj = lax.broadcasted_iota(I32, (t, t), 0)
        r_ref[...] = (qi - kj).astype(F32) * slope2

    qt = (q_ref[0].astype(F32) * (scale * LOG2E)).T
    row = lax.broadcasted_iota(I32, qt.shape, 0)
    qz_ref[0] = jnp.where(row < ATT_HALF_DIM, qt, 0.0).astype(BF16)
    qz_ref[1] = jnp.where(row >= ATT_HALF_DIM, qt, 0.0).astype(BF16)
    m_ref[...] = jnp.full(m_ref.shape, NEG_BIG, F32)
    l_ref[...] = jnp.zeros(l_ref.shape, F32)
    acc_ref[...] = jnp.zeros(acc_ref.shape, F32)

    def tile(jk, mode):
        kt = k_ref[0, pl.ds(pl.multiple_of(jk * t, t), t), :]
        vt = vt_ref[jk]
        c = -(slope2 * t) * jnp.abs(iq - jk).astype(F32)
        r = r_ref[...]
        for mp in (0, 1):
            s = jnp.dot(kt, qz_ref[mp], preferred_element_type=F32)
            if mode == 0:
                s = s - r
            elif mode == 2:
                s = s + r
            else:
                s = s - jnp.abs(r)
            m_old = m_ref[mp]
            m_new = jnp.maximum(m_old, jnp.max(s, axis=0, keepdims=True) + c)
            a = jnp.exp2(m_old - m_new)
            p = jnp.exp2(s - (m_new - c))
            l_ref[mp] = a * l_ref[mp] + jnp.sum(p, axis=0, keepdims=True)
            acc_ref[mp] = a * acc_ref[mp] + jnp.dot(vt, p.astype(BF16), preferred_element_type=F32)
            m_ref[mp] = m_new

    def left(j, carry):
        tile(j, 0)
        return carry

    def right(j, carry):
        tile(j, 2)
        return carry

    lax.fori_loop(0, iq, left, 0)
    tile(iq, 1)
    lax.fori_loop(iq + 1, nk, right, 0)

    lv = lamv_ref[...]
    s1 = jnp.sum(lv[0:1] * lv[1:2], axis=-1, keepdims=True)
    s2 = jnp.sum(lv[2:3] * lv[3:4], axis=-1, keepdims=True)
    lam = jnp.exp(s1) - jnp.exp(s2) + lam_init
    o = acc_ref[0] * (1.0 / l_ref[0]) - lam * (acc_ref[1] * (1.0 / l_ref[1]))
    ms = jnp.mean(o * o, axis=0, keepdims=True)
    on = o * lax.rsqrt(ms + LN_EPS) * g_ref[...] * (1.0 - lam_init)
    o_ref[0] = on.T.astype(o_ref.dtype)


def _attention(proj3, lamv, subln_g, n_heads, lam_init, t):
    b, s, _ = proj3.shape
    nk = s // t
    vd = ATT_V_DIM
    slopes = jnp.asarray(2.0 ** (-8.0 * np.arange(1, n_heads + 1) / n_heads), dtype=F32)
    kern = functools.partial(_attn_kernel, t=t, nk=nk, lam_init=lam_init, scale=ATT_HALF_DIM ** -0.5)
    grid_spec = pltpu.PrefetchScalarGridSpec(
        num_scalar_prefetch=1, grid=(b, n_heads, nk),
        in_specs=[
            pl.BlockSpec((1, t, vd), lambda bi, hi, qi, sl: (bi, qi, hi)),
            pl.BlockSpec((1, s, vd), lambda bi, hi, qi, sl: (bi, 0, n_heads + hi)),
            pl.BlockSpec((1, s, vd), lambda bi, hi, qi, sl: (bi, 0, 2 * n_heads + hi)),
            pl.BlockSpec((4, ATT_HALF_DIM), lambda bi, hi, qi, sl: (0, 0)),
            pl.BlockSpec((vd, 1), lambda bi, hi, qi, sl: (0, 0)),
        ],
        out_specs=pl.BlockSpec((1, t, vd), lambda bi, hi, qi, sl: (bi, qi, hi)),
        scratch_shapes=[
            pltpu.VMEM((nk, vd, t), BF16),
            pltpu.VMEM((2, vd, t), BF16),
            pltpu.VMEM((t, t), F32),
            pltpu.VMEM((2, 1, t), F32),
            pltpu.VMEM((2, 1, t), F32),
            pltpu.VMEM((2, vd, t), F32),
        ],
    )
    return pl.pallas_call(
        kern, grid_spec=grid_spec,
        out_shape=jax.ShapeDtypeStruct((b, s, n_heads * vd), BF16),
        compiler_params=_cparams(("parallel", "parallel", "arbitrary"), 48), name="diff_attn",
    )(slopes, proj3, proj3, proj3, lamv, subln_g.reshape(vd, 1))


def _conv_kernel(vc_ref, gc_ref, vp_ref, gp_ref, vn_ref, gn_ref, w_ref, b_ref, lg_ref, lb_ref,
                 o_ref, ext_ref, *, ts, width, rc):
    i = pl.program_id(1)
    n = pl.num_programs(1)

    def glu(v_ref, g_ref):
        return v_ref[0].astype(F32) * jax.nn.sigmoid(g_ref[0].astype(F32))

    hp = CONV_HALO
    ext_ref[0:hp, :] = glu(vp_ref, gp_ref) * (i > 0).astype(F32)
    ext_ref[hp:hp + ts, :] = glu(vc_ref, gc_ref)
    ext_ref[hp + ts:hp + ts + hp, :] = glu(vn_ref, gn_ref) * (i < n - 1).astype(F32)
    pad = width // 2
    c = o_ref.shape[-1]
    for r0 in range(0, ts, rc):
        acc = jnp.zeros((rc, c), F32) + b_ref[...]
        for j in range(width):
            off = r0 + hp - pad + j
            acc = acc + ext_ref[off:off + rc, :] * w_ref[j:j + 1, :]
        y = _ln_rows(acc, lg_ref[...], lb_ref[...])
        o_ref[0, r0:r0 + rc, :] = (y * jax.nn.sigmoid(y)).astype(o_ref.dtype)


def _conv_branch(proj3, w_dw, b_dw, g_ln, b_ln, col_val, ts):
    b, s, _ = proj3.shape
    width, c = w_dw.shape
    assert width // 2 <= CONV_HALO and col_val % c == 0 and ts % CONV_HALO == 0
    iv = col_val // c
    hb = ts // CONV_HALO
    nhb = s // CONV_HALO
    cur = lambda off: pl.BlockSpec((1, ts, c), lambda bi, i: (bi, i, iv + off))
    prev = lambda off: pl.BlockSpec((1, CONV_HALO, c), lambda bi, i: (bi, jnp.maximum(i * hb - 1, 0), iv + off))
    nxt = lambda off: pl.BlockSpec((1, CONV_HALO, c), lambda bi, i: (bi, jnp.minimum((i + 1) * hb, nhb - 1), iv + off))
    vec = pl.BlockSpec((1, c), lambda bi, i: (0, 0))
    kern = functools.partial(_conv_kernel, ts=ts, width=width, rc=32)
    return pl.pallas_call(
        kern, grid=(b, s // ts),
        in_specs=[cur(0), cur(1), prev(0), prev(1), nxt(0), nxt(1),
                  pl.BlockSpec((width, c), lambda bi, i: (0, 0)), vec, vec, vec],
        out_specs=pl.BlockSpec((1, ts, c), lambda bi, i: (bi, i, 0)),
        out_shape=jax.ShapeDtypeStruct((b, s, c), BF16),
        scratch_shapes=[pltpu.VMEM((ts + 2 * CONV_HALO, c), F32)],
        compiler_params=_cparams(("parallel", "parallel"), 48), name="conv_branch",
    )(proj3, proj3, proj3, proj3, proj3, proj3, w_dw, b_dw.reshape(1, c), g_ln.reshape(1, c), b_ln.reshape(1, c))


def _merge_kernel(o_ref, c_ref, wpa_ref, wpc_ref, bpc_ref, ga_ref, gc_ref, bga_ref, bgc_ref, out_ref):
    ya = jnp.dot(o_ref[...], wpa_ref[...], preferred_element_type=F32)
    yc = jnp.dot(c_ref[...], wpc_ref[...], preferred_element_type=F32) + bpc_ref[...]
    g_att = jax.nn.sigmoid(ga_ref[...].astype(F32) + bga_ref[...])
    g_conv = jax.nn.sigmoid(gc_ref[...].astype(F32) + bgc_ref[...])
    out_ref[...] = (g_att * ya + g_conv * yc).astype(out_ref.dtype)


def _merge(o2, c2, w_pa, w_pc, b_pc, proj2, b_gate, col_gate, tm, tn):
    n, da = o2.shape
    dc = c2.shape[1]
    d = w_pa.shape[1]
    assert col_gate % tn == 0 and d % tn == 0
    ig = col_gate // tn
    nj = d // tn
    vec = lambda off: pl.BlockSpec((1, tn), lambda i, j: (0, off + j))
    return pl.pallas_call(
        _merge_kernel, grid=(n // tm, nj),
        in_specs=[pl.BlockSpec((tm, da), lambda i, j: (i, 0)), pl.BlockSpec((tm, dc), lambda i, j: (i, 0)),
                  pl.BlockSpec((da, tn), lambda i, j: (0, j)), pl.BlockSpec((dc, tn), lambda i, j: (0, j)), vec(0),
                  pl.BlockSpec((tm, tn), lambda i, j: (i, ig + j)),
                  pl.BlockSpec((tm, tn), lambda i, j: (i, ig + nj + j)),
                  vec(0), vec(nj)],
        out_specs=pl.BlockSpec((tm, tn), lambda i, j: (i, j)),
        out_shape=jax.ShapeDtypeStruct((n, d), BF16),
        compiler_params=_cparams(("parallel", "parallel"), 48), name="merge",
    )(o2, c2, w_pa, w_pc, b_pc.reshape(1, d), proj2, proj2, b_gate.reshape(1, 2 * d), b_gate.reshape(1, 2 * d))


def _outproj_ln_kernel(m_ref, w_ref, b_ref, h_ref, g_ref, be_ref, o_ref, *, alpha):
    mix = jnp.dot(m_ref[...], w_ref[...], preferred_element_type=F32) + b_ref[...]
    o_ref[...] = _ln_rows(alpha * h_ref[...] + mix, g_ref[...], be_ref[...])


def _outproj_ln(merged, w_out, b_out, h, g, b, alpha, tm):
    n, d = h.shape
    row = pl.BlockSpec((tm, d), lambda i: (i, 0))
    vec = pl.BlockSpec((1, d), lambda i: (0, 0))
    return pl.pallas_call(
        functools.partial(_outproj_ln_kernel, alpha=alpha), grid=(n // tm,),
        in_specs=[row, pl.BlockSpec((d, d), lambda i: (0, 0)), vec, row, vec, vec],
        out_specs=row, out_shape=jax.ShapeDtypeStruct((n, d), F32),
        compiler_params=_cparams(("parallel",), 56), name="outproj_ln",
    )(merged, w_out, b_out.reshape(1, d), h, g.reshape(1, d), b.reshape(1, d))


def _router_kernel(h_ref, wr_ref, br_ref, idx_ref, gate_ref, rank_ref, cnt_ref, carry_ref, *, n_exp, t):
    i = pl.program_id(0)

    @pl.when(i == 0)
    def _():
        carry_ref[...] = jnp.zeros(carry_ref.shape, F32)

    logits = lax.dot_general(wr_ref[...], h_ref[...], (((1,), (1,)), ((), ())),
                             precision=lax.Precision.HIGHEST, preferred_element_type=F32) + br_ref[...]
    iota_e = lax.broadcasted_iota(I32, (n_exp, t), 0).astype(F32)
    work = logits
    member = jnp.zeros((n_exp, t), F32)
    vals, idxs = [], []
    for _ in range(TOP_K):
        mx = jnp.max(work, axis=0, keepdims=True)
        idx = jnp.min(jnp.where(work == mx, iota_e, float(n_exp)), axis=0, keepdims=True)
        sel = iota_e == idx
        vals.append(mx)
        idxs.append(idx)
        member = member + sel.astype(F32)
        work = jnp.where(sel, -jnp.inf, work)
    es = [jnp.exp(v - vals[0]) for v in vals]
    inv = 1.0 / functools.reduce(lambda a, b: a + b, es)
    earlier = (lax.broadcasted_iota(I32, (t, t), 0) < lax.broadcasted_iota(I32, (t, t), 1)).astype(BF16)
    carry = carry_ref[...]
    excl = jnp.dot(member.astype(BF16), earlier, preferred_element_type=F32) + carry
    for k in range(TOP_K):
        idx_ref[k:k + 1, :] = idxs[k].astype(I32)
        gate_ref[k:k + 1, :] = es[k] * inv
        rank_ref[k:k + 1, :] = jnp.sum(jnp.where(iota_e == idxs[k], excl, 0.0), axis=0, keepdims=True).astype(I32)
    carry_new = carry + jnp.sum(member, axis=1, keepdims=True)
    carry_ref[...] = carry_new
    cnt_ref[...] = jnp.broadcast_to(carry_new, cnt_ref.shape).astype(I32)


def _router(h, w_router, b_router, t):
    n, d = h.shape
    e = w_router.shape[1]
    out_tok = pl.BlockSpec((TOP_K, t), lambda i: (0, i))
    return pl.pallas_call(
        functools.partial(_router_kernel, n_exp=e, t=t), grid=(n // t,),
        in_specs=[pl.BlockSpec((t, d), lambda i: (i, 0)), pl.BlockSpec((e, d), lambda i: (0, 0)),
                  pl.BlockSpec((e, 1), lambda i: (0, 0))],
        out_specs=[out_tok, out_tok, out_tok, pl.BlockSpec((e, 128), lambda i: (0, 0))],
        out_shape=[jax.ShapeDtypeStruct((TOP_K, n), I32), jax.ShapeDtypeStruct((TOP_K, n), F32),
                   jax.ShapeDtypeStruct((TOP_K, n), I32), jax.ShapeDtypeStruct((e, 128), I32)],
        scratch_shapes=[pltpu.VMEM((e, 1), F32)],
        compiler_params=_cparams(("arbitrary",), 48), name="router",
    )(h, w_router.T, b_router.reshape(e, 1))


def _plan_kernel(cnt_ref, idx_ref, rank_ref, pos_ref, be_ref, nu_ref, *, n_exp, tm_log2, tn, nbp):
    tm = 1 << tm_log2
    cnt = cnt_ref[:, 0:n_exp]
    padded = (((cnt + (tm - 1)) >> tm_log2) << tm_log2).astype(F32)
    row = lax.broadcasted_iota(I32, (n_exp, n_exp), 0)
    col = lax.broadcasted_iota(I32, (n_exp, n_exp), 1)
    padded_row = jnp.sum(jnp.where(row == col, padded, 0.0), axis=0, keepdims=True)
    pstart = jnp.sum(jnp.where(col < row, padded_row, 0.0), axis=1, keepdims=True)
    pend = pstart + padded[:, 0:1]
    iota_e = lax.broadcasted_iota(I32, (n_exp, tn), 0)
    for k in range(TOP_K):
        hit = iota_e == idx_ref[k:k + 1, :]
        pos_ref[k:k + 1, :] = (jnp.sum(jnp.where(hit, pstart, 0.0), axis=0, keepdims=True).astype(I32)
                               + rank_ref[k:k + 1, :])
    blk_start = (lax.broadcasted_iota(I32, (n_exp, nbp), 1) * tm).astype(F32)
    be = jnp.sum((pend <= blk_start).astype(F32), axis=0, keepdims=True)
    be_ref[...] = jnp.minimum(be, n_exp - 1.0).astype(I32)
    total = jnp.max(pend, axis=0, keepdims=True)
    nu_ref[...] = jnp.broadcast_to(total, nu_ref.shape).astype(I32) >> tm_log2


def _plan(cnt, idx, rank, tm, nb, tn):
    e = cnt.shape[0]
    n = idx.shape[1]
    nbp = ((nb + 127) // 128) * 128
    tm_log2 = int(math.log2(tm))
    assert 1 << tm_log2 == tm
    tok = pl.BlockSpec((TOP_K, tn), lambda i: (0, i))
    return pl.pallas_call(
        functools.partial(_plan_kernel, n_exp=e, tm_log2=tm_log2, tn=tn, nbp=nbp), grid=(n // tn,),
        in_specs=[pl.BlockSpec((e, 128), lambda i: (0, 0)), tok, tok],
        out_specs=[tok, pl.BlockSpec((1, nbp), lambda i: (0, 0)), pl.BlockSpec((1, 128), lambda i: (0, 0))],
        out_shape=[jax.ShapeDtypeStruct((TOP_K, n), I32), jax.ShapeDtypeStruct((1, nbp), I32),
                   jax.ShapeDtypeStruct((1, 128), I32)],
        compiler_params=_cparams(("arbitrary",), 48), name="plan",
    )(cnt, idx, rank)


def _dispatch_kernel(pos_ref, h_ref, xs_in_ref, xs_ref, sem, *, t, n):
    del xs_in_ref
    base = pl.program_id(0) * t
    for k in range(TOP_K):
        def body(j, carry, k=k):
            r = pos_ref[k * n + base + j]
            pltpu.make_async_copy(h_ref.at[pl.ds(j, 1)], xs_ref.at[pl.ds(r, 1)], sem).start()
            return carry
        lax.fori_loop(0, t, body, 0)
    for k in range(TOP_K):
        pltpu.make_async_copy(h_ref, xs_ref.at[pl.ds(0, t)], sem).wait()


def _dispatch(pos_flat, h, n_rows, t):
    n, d = h.shape
    grid_spec = pltpu.PrefetchScalarGridSpec(
        num_scalar_prefetch=1, grid=(n // t,),
        in_specs=[pl.BlockSpec((t, d), lambda i, p: (i, 0)), pl.BlockSpec(memory_space=pl.ANY)],
        out_specs=pl.BlockSpec(memory_space=pl.ANY),
        scratch_shapes=[pltpu.SemaphoreType.DMA(())],
    )
    return pl.pallas_call(
        functools.partial(_dispatch_kernel, t=t, n=n), grid_spec=grid_spec,
        out_shape=jax.ShapeDtypeStruct((n_rows, d), h.dtype),
        input_output_aliases={2: 0},
        compiler_params=_cparams(("arbitrary",), 48), name="dispatch",
    )(pos_flat, h, jnp.zeros((n_rows, d), h.dtype))


def _gu_kernel(be_ref, nu_ref, x_ref, w_ref, b_ref, o_ref, *, f):
    @pl.when(pl.program_id(0) < nu_ref[0])
    def _():
        gu = jnp.dot(x_ref[...].astype(BF16), w_ref[0], preferred_element_type=F32) + b_ref[0]
        g = jnp.minimum(gu[:, :f], SWIGLU_LIMIT)
        u = jnp.clip(gu[:, f:], -SWIGLU_LIMIT, SWIGLU_LIMIT)
        o_ref[...] = (g * jax.nn.sigmoid(SWIGLU_ALPHA * g) * (u + 1.0)).astype(o_ref.dtype)

    @pl.when(pl.program_id(0) >= nu_ref[0])
    def _():
        o_ref[...] = jnp.zeros(o_ref.shape, o_ref.dtype)


def _dn_kernel(be_ref, nu_ref, x_ref, w_ref, b_ref, o_ref):
    @pl.when(pl.program_id(0) < nu_ref[0])
    def _():
        o_ref[...] = jnp.dot(x_ref[...], w_ref[0], preferred_element_type=F32) + b_ref[0]

    @pl.when(pl.program_id(0) >= nu_ref[0])
    def _():
        o_ref[...] = jnp.zeros(o_ref.shape, o_ref.dtype)


def _expert_matmul(kern, be, nu, x, w, b, out_cols, out_dtype, tm, name):
    n_rows, kd = x.shape
    e, _, m = w.shape
    blk = lambda i, be_r, nu_r: jnp.minimum(i, nu_r[0] - 1)
    grid_spec = pltpu.PrefetchScalarGridSpec(
        num_scalar_prefetch=2, grid=(n_rows // tm,),
        in_specs=[pl.BlockSpec((tm, kd), lambda i, be_r, nu_r: (blk(i, be_r, nu_r), 0)),
                  pl.BlockSpec((1, kd, m), lambda i, be_r, nu_r: (be_r[blk(i, be_r, nu_r)], 0, 0)),
                  pl.BlockSpec((1, 1, m), lambda i, be_r, nu_r: (be_r[blk(i, be_r, nu_r)], 0, 0))],
        out_specs=pl.BlockSpec((tm, out_cols), lambda i, be_r, nu_r: (i, 0)),
    )
    return pl.pallas_call(
        kern, grid_spec=grid_spec, out_shape=jax.ShapeDtypeStruct((n_rows, out_cols), out_dtype),
        compiler_params=_cparams(("arbitrary",), 58), name=name,
    )(be, nu, x, w, b.reshape(e, 1, m))


def _combine_kernel(pos_ref, y_ref, h_ref, gt_ref, g_ref, b_ref, o_ref, ob_ref, ybuf, sem, *, t, n, alpha):
    i = pl.program_id(0)
    nsteps = pl.num_programs(0)

    def issue(step, slot):
        base = step * t
        for k in range(TOP_K):
            def body(j, carry, k=k):
                r = pos_ref[k * n + base + j]
                pltpu.make_async_copy(y_ref.at[pl.ds(r, 1)], ybuf.at[slot, k, pl.ds(j, 1)], sem.at[slot]).start()
                return carry
            lax.fori_loop(0, t, body, 0)

    @pl.when(i == 0)
    def _():
        issue(0, 0)

    slot = i % 2

    @pl.when(i + 1 < nsteps)
    def _():
        issue(i + 1, 1 - slot)

    for k in range(TOP_K):
        pltpu.make_async_copy(y_ref.at[pl.ds(0, t)], ybuf.at[slot, k], sem.at[slot]).wait()
    gt = gt_ref[...]
    ff = gt[:, 0:1] * ybuf[slot, 0]
    for k in range(1, TOP_K):
        ff = ff + gt[:, k:k + 1] * ybuf[slot, k]
    out = _ln_rows(alpha * h_ref[...] + ff, g_ref[...], b_ref[...])
    o_ref[...] = out
    ob_ref[...] = out.astype(BF16)


def _combine(pos_flat, y, h, gates_t, g, b, alpha, t):
    n, d = h.shape
    row = pl.BlockSpec((t, d), lambda i, p: (i, 0))
    vec = pl.BlockSpec((1, d), lambda i, p: (0, 0))
    grid_spec = pltpu.PrefetchScalarGridSpec(
        num_scalar_prefetch=1, grid=(n // t,),
        in_specs=[pl.BlockSpec(memory_space=pl.ANY), row, pl.BlockSpec((t, TOP_K), lambda i, p: (i, 0)), vec, vec],
        out_specs=[row, row],
        scratch_shapes=[pltpu.VMEM((2, TOP_K, t, d), F32), pltpu.SemaphoreType.DMA((2,))],
    )
    return pl.pallas_call(
        functools.partial(_combine_kernel, t=t, n=n, alpha=alpha), grid_spec=grid_spec,
        out_shape=[jax.ShapeDtypeStruct((n, d), F32), jax.ShapeDtypeStruct((n, d), BF16)],
        compiler_params=_cparams(("arbitrary",), 48), name="combine",
    )(pos_flat, y, h, gates_t, g.reshape(1, d), b.reshape(1, d))


def kernel(x, ln_in_g, ln_in_b, w_in, b_gate, lambda_q1, lambda_k1, lambda_q2, lambda_k2, subln_g, w_dw, b_dw,
           conv_ln_g, conv_ln_b, w_pa, w_pc, b_pc, w_out, b_out, ln1_g, ln1_b, w_router, b_router, w_gu, b_gu,
           w_dn, b_dn, ln2_g, ln2_b):
    bsz, seq, d = x.shape
    depth = w_in.shape[0]
    d_att = w_pa.shape[1]
    d_conv = w_pc.shape[1]
    n_heads = d_att // ATT_V_DIM
    n_exp = w_router.shape[-1]
    d_ff = w_dn.shape[-2]
    n = bsz * seq
    alpha = (2 * depth) ** 0.25

    tm_rows = min(512, n)
    t_att = min(512, seq)
    ts_conv = min(512, seq)
    tm_moe = min(256, n)
    t_router = min(512, n)
    t_disp = min(256, n)
    t_comb = min(128, n)
    n_rows = n * TOP_K + n_exp * tm_moe
    nb = n_rows // tm_moe

    h, hb = _layer_norm_in(x.reshape(n, d), ln_in_g, ln_in_b, tm_rows)
    for l in range(depth):
        lam_init = 0.8 - 0.6 * math.exp(-0.3 * l)
        w_in_b = w_in[l].astype(BF16)
        d_in = w_in_b.shape[1]
        proj = _matmul(hb, w_in_b, _tile(n, 1024), _tile(d_in, 1024), BF16, "in_proj")
        proj3 = proj.reshape(bsz, seq, d_in)
        lamv = jnp.stack([lambda_q1[l], lambda_k1[l], lambda_q2[l], lambda_k2[l]]).astype(F32)
        o = _attention(proj3, lamv, subln_g[l], n_heads, lam_init, t_att)
        c = _conv_branch(proj3, w_dw[l].reshape(w_dw.shape[1], d_conv), b_dw[l], conv_ln_g[l], conv_ln_b[l],
                         3 * d_att, ts_conv)
        merged = _merge(o.reshape(n, d_att), c.reshape(n, d_conv), w_pa[l].astype(BF16), w_pc[l].astype(BF16),
                        b_pc[l], proj, b_gate[l], 3 * d_att + 2 * d_conv, tm_rows, min(1024, d))
        h1 = _outproj_ln(merged, w_out[l].astype(BF16), b_out[l], h, ln1_g[l], ln1_b[l], alpha, min(256, n))

        idx, gates, rank, cnt = _router(h1, w_router[l], b_router[l], t_router)
        pos, be, nu = _plan(cnt, idx, rank, tm_moe, nb, min(2048, n))
        pos_flat = pos.reshape(TOP_K * n)
        be_flat = be.reshape(-1)
        nu_flat = nu.reshape(-1)[:1]
        xs = _dispatch(pos_flat, h1, n_rows, t_disp)
        hid = _expert_matmul(functools.partial(_gu_kernel, f=d_ff), be_flat, nu_flat, xs, w_gu[l].astype(BF16),
                             b_gu[l], d_ff, BF16, tm_moe, "expert_gate_up")
        y = _expert_matmul(_dn_kernel, be_flat, nu_flat, hid, w_dn[l].astype(BF16), b_dn[l], d, F32, tm_moe,
                           "expert_down")
        h, hb = _combine(pos_flat, y, h1, gates.T, ln2_g[l], ln2_b[l], alpha, t_comb)
    return h.reshape(bsz, seq, d)
```

```python
import functools
import math

import numpy as np
import jax
import jax.numpy as jnp
from jax import lax
from jax.experimental import pallas as pl
from jax.experimental.pallas import tpu as pltpu

F32 = jnp.float32
BF16 = jnp.bfloat16
I32 = jnp.int32

LN_EPS = 1e-5
ATT_HALF_DIM = 64
ATT_V_DIM = 2 * ATT_HALF_DIM
TOP_K = 4
SWIGLU_LIMIT = 7.0
SWIGLU_ALPHA = 1.702
LOG2E = 1.4426950408889634
CONV_HALO = 16
MIB = 1024 * 1024
NEG_BIG = -1e30


def _cparams(semantics, vmem_mib):
    return pltpu.CompilerParams(dimension_semantics=semantics, vmem_limit_bytes=vmem_mib * MIB)


def _tile(total, target, quantum=128):
    if total <= target:
        return total
    t = (target // quantum) * quantum
    while total % t:
        t -= quantum
    return t


def _ln_rows(x, g, b):
    mu = jnp.mean(x, axis=-1, keepdims=True)
    xc = x - mu
    var = jnp.mean(xc * xc, axis=-1, keepdims=True)
    return xc * lax.rsqrt(var + LN_EPS) * g + b


def _ln_kernel(x_ref, g_ref, b_ref, o_ref, ob_ref):
    y = _ln_rows(x_ref[...], g_ref[...], b_ref[...])
    o_ref[...] = y
    ob_ref[...] = y.astype(BF16)


def _layer_norm_in(x2, g, b, tm):
    n, d = x2.shape
    row = pl.BlockSpec((tm, d), lambda i: (i, 0))
    vec = pl.BlockSpec((1, d), lambda i: (0, 0))
    return pl.pallas_call(
        _ln_kernel, grid=(n // tm,), in_specs=[row, vec, vec], out_specs=[row, row],
        out_shape=[jax.ShapeDtypeStruct((n, d), F32), jax.ShapeDtypeStruct((n, d), BF16)],
        compiler_params=_cparams(("parallel",), 48), name="ln_in",
    )(x2, g.reshape(1, d), b.reshape(1, d))


def _mm_kernel(x_ref, w_ref, o_ref):
    o_ref[...] = jnp.dot(x_ref[...], w_ref[...], preferred_element_type=F32).astype(o_ref.dtype)


def _matmul(x, w, tm, tn, out_dtype, name):
    n, kd = x.shape
    m = w.shape[1]
    return pl.pallas_call(
        _mm_kernel, grid=(n // tm, m // tn),
        in_specs=[pl.BlockSpec((tm, kd), lambda i, j: (i, 0)), pl.BlockSpec((kd, tn), lambda i, j: (0, j))],
        out_specs=pl.BlockSpec((tm, tn), lambda i, j: (i, j)),
        out_shape=jax.ShapeDtypeStruct((n, m), out_dtype),
        compiler_params=_cparams(("parallel", "parallel"), 48), name=name,
    )(x, w)


def _attn_kernel(slopes_ref, q_ref, k_ref, v_ref, lamv_ref, g_ref, o_ref,
                 vt_ref, qz_ref, r_ref, sa_ref, sb_ref, p_ref, m_ref, l_ref, acc_ref, *, t, nk, lam_init, scale):
    h = pl.program_id(1)
    iq = pl.program_id(2)
    slope2 = slopes_ref[h] * LOG2E

    @pl.when(iq == 0)
    def _():
        for j in range(nk):
            vt_ref[j] = v_ref[0, j * t:(j + 1) * t, :].astype(F32).T.astype(BF16)
        qi = lax.broadcasted_iota(I32, (t, t), 1)
        kj = lax.broadcasted_iota(I32, (t, t), 0)
        r = (qi - kj).astype(F32) * slope2
        r_ref[0] = -r
        r_ref[1] = -jnp.abs(r)
        r_ref[2] = r

    qt = (q_ref[0].astype(F32) * (scale * LOG2E)).T
    row = lax.broadcasted_iota(I32, qt.shape, 0)
    qz_ref[0] = jnp.where(row < ATT_HALF_DIM, qt, 0.0).astype(BF16)
    qz_ref[1] = jnp.where(row >= ATT_HALF_DIM, qt, 0.0).astype(BF16)
    m_ref[...] = jnp.full(m_ref.shape, NEG_BIG, F32)
    l_ref[...] = jnp.zeros(l_ref.shape, F32)
    acc_ref[...] = jnp.zeros(acc_ref.shape, F32)

    def tile_mode(jk):
        return jnp.where(jk < iq, 0, jnp.where(jk == iq, 1, 2))

    def scores(jk, s_ref):
        kt = k_ref[0, pl.ds(pl.multiple_of(jk * t, t), t), :]
        r = r_ref[tile_mode(jk)]
        for mp in (0, 1):
            s_ref[mp] = jnp.dot(kt, qz_ref[mp], preferred_element_type=F32) + r

    def update(jk, s_ref):
        c = -(slope2 * t) * jnp.abs(iq - jk).astype(F32)
        vt = vt_ref[jk]
        for mp in (0, 1):
            s = s_ref[mp]
            m_old = m_ref[mp]
            m_new = jnp.maximum(m_old, jnp.max(s, axis=0, keepdims=True) + c)
            a = jnp.exp2(m_old - m_new)
            p = jnp.exp2(s - (m_new - c))
            l_ref[mp] = a * l_ref[mp] + jnp.sum(p, axis=0, keepdims=True)
            m_ref[mp] = m_new
            p_ref[...] = p.astype(BF16)
            acc_ref[mp] = a * acc_ref[mp] + jnp.dot(vt, p_ref[...], preferred_element_type=F32)

    scores(0, sa_ref)

    def pair(jj, carry):
        j0 = 2 * jj
        scores(j0 + 1, sb_ref)
        update(j0, sa_ref)
        scores(jnp.minimum(j0 + 2, nk - 1), sa_ref)
        update(j0 + 1, sb_ref)
        return carry

    lax.fori_loop(0, nk // 2, pair, 0)
    if nk % 2:
        update(nk - 1, sa_ref)

    lv = lamv_ref[...]
    s1 = jnp.sum(lv[0:1] * lv[1:2], axis=-1, keepdims=True)
    s2 = jnp.sum(lv[2:3] * lv[3:4], axis=-1, keepdims=True)
    lam = jnp.exp(s1) - jnp.exp(s2) + lam_init
    o = acc_ref[0] * (1.0 / l_ref[0]) - lam * (acc_ref[1] * (1.0 / l_ref[1]))
    ms = jnp.mean(o * o, axis=0, keepdims=True)
    on = o * lax.rsqrt(ms + LN_EPS) * g_ref[...] * (1.0 - lam_init)
    o_ref[0] = on.T.astype(o_ref.dtype)


def _attention(proj3, lamv, subln_g, n_heads, lam_init, t):
    b, s, _ = proj3.shape
    nk = s // t
    vd = ATT_V_DIM
    slopes = jnp.asarray(2.0 ** (-8.0 * np.arange(1, n_heads + 1) / n_heads), dtype=F32)
    kern = functools.partial(_attn_kernel, t=t, nk=nk, lam_init=lam_init, scale=ATT_HALF_DIM ** -0.5)
    grid_spec = pltpu.PrefetchScalarGridSpec(
        num_scalar_prefetch=1, grid=(b, n_heads, nk),
        in_specs=[
            pl.BlockSpec((1, t, vd), lambda bi, hi, qi, sl: (bi, qi, hi)),
            pl.BlockSpec((1, s, vd), lambda bi, hi, qi, sl: (bi, 0, n_heads + hi)),
            pl.BlockSpec((1, s, vd), lambda bi, hi, qi, sl: (bi, 0, 2 * n_heads + hi)),
            pl.BlockSpec((4, ATT_HALF_DIM), lambda bi, hi, qi, sl: (0, 0)),
            pl.BlockSpec((vd, 1), lambda bi, hi, qi, sl: (0, 0)),
        ],
        out_specs=pl.BlockSpec((1, t, vd), lambda bi, hi, qi, sl: (bi, qi, hi)),
        scratch_shapes=[
            pltpu.VMEM((nk, vd, t), BF16),
            pltpu.VMEM((2, vd, t), BF16),
            pltpu.VMEM((3, t, t), F32),
            pltpu.VMEM((2, t, t), F32),
            pltpu.VMEM((2, t, t), F32),
            pltpu.VMEM((t, t), BF16),
            pltpu.VMEM((2, 1, t), F32),
            pltpu.VMEM((2, 1, t), F32),
            pltpu.VMEM((2, vd, t), F32),
        ],
    )
    return pl.pallas_call(
        kern, grid_spec=grid_spec,
        out_shape=jax.ShapeDtypeStruct((b, s, n_heads * vd), BF16),
        compiler_params=_cparams(("parallel", "parallel", "arbitrary"), 48), name="diff_attn",
    )(slopes, proj3, proj3, proj3, lamv, subln_g.reshape(vd, 1))


def _conv_kernel(vc_ref, gc_ref, vp_ref, gp_ref, vn_ref, gn_ref, w_ref, b_ref, lg_ref, lb_ref,
                 o_ref, ext_ref, sh_ref, *, ts, width, rc):
    i = pl.program_id(1)
    n = pl.num_programs(1)

    def glu(v_ref, g_ref):
        return v_ref[0].astype(F32) * jax.nn.sigmoid(g_ref[0].astype(F32))

    hp = CONV_HALO
    ext_ref[0:hp, :] = glu(vp_ref, gp_ref) * (i > 0).astype(F32)
    ext_ref[hp:hp + ts, :] = glu(vc_ref, gc_ref)
    ext_ref[hp + ts:hp + ts + hp, :] = glu(vn_ref, gn_ref) * (i < n - 1).astype(F32)
    rows = sh_ref.shape[1]
    for s in range(8):
        sh_ref[s] = ext_ref[s:s + rows, :]
    pad = width // 2
    c = o_ref.shape[-1]
    for r0 in range(0, ts, rc):
        acc = jnp.zeros((rc, c), F32) + b_ref[...]
        for j in range(width):
            off = r0 + hp - pad + j
            acc = acc + sh_ref[off % 8, off - off % 8:off - off % 8 + rc, :] * w_ref[j:j + 1, :]
        y = _ln_rows(acc, lg_ref[...], lb_ref[...])
        o_ref[0, r0:r0 + rc, :] = (y * jax.nn.sigmoid(y)).astype(o_ref.dtype)


def _conv_branch(proj3, w_dw, b_dw, g_ln, b_ln, col_val, ts):
    b, s, _ = proj3.shape
    width, c = w_dw.shape
    assert width // 2 <= CONV_HALO and col_val % c == 0 and ts % CONV_HALO == 0
    iv = col_val // c
    hb = ts // CONV_HALO
    nhb = s // CONV_HALO
    cur = lambda off: pl.BlockSpec((1, ts, c), lambda bi, i: (bi, i, iv + off))
    prev = lambda off: pl.BlockSpec((1, CONV_HALO, c), lambda bi, i: (bi, jnp.maximum(i * hb - 1, 0), iv + off))
    nxt = lambda off: pl.BlockSpec((1, CONV_HALO, c), lambda bi, i: (bi, jnp.minimum((i + 1) * hb, nhb - 1), iv + off))
    vec = pl.BlockSpec((1, c), lambda bi, i: (0, 0))
    kern = functools.partial(_conv_kernel, ts=ts, width=width, rc=32)
    return pl.pallas_call(
        kern, grid=(b, s // ts),
        in_specs=[cur(0), cur(1), prev(0), prev(1), nxt(0), nxt(1),
                  pl.BlockSpec((width, c), lambda bi, i: (0, 0)), vec, vec, vec],
        out_specs=pl.BlockSpec((1, ts, c), lambda bi, i: (bi, i, 0)),
        out_shape=jax.ShapeDtypeStruct((b, s, c), BF16),
        scratch_shapes=[pltpu.VMEM((ts + 2 * CONV_HALO, c), F32), pltpu.VMEM((8, ts + CONV_HALO + 8, c), F32)],
        compiler_params=_cparams(("parallel", "parallel"), 48), name="conv_branch",
    )(proj3, proj3, proj3, proj3, proj3, proj3, w_dw, b_dw.reshape(1, c), g_ln.reshape(1, c), b_ln.reshape(1, c))


def _merge_kernel(o_ref, c_ref, wpa_ref, wpc_ref, bpc_ref, ga_ref, gc_ref, bga_ref, bgc_ref, out_ref):
    ya = jnp.dot(o_ref[...], wpa_ref[...], preferred_element_type=F32)
    yc = jnp.dot(c_ref[...], wpc_ref[...], preferred_element_type=F32) + bpc_ref[...]
    g_att = jax.nn.sigmoid(ga_ref[...].astype(F32) + bga_ref[...])
    g_conv = jax.nn.sigmoid(gc_ref[...].astype(F32) + bgc_ref[...])
    out_ref[...] = (g_att * ya + g_conv * yc).astype(out_ref.dtype)


def _merge(o2, c2, w_pa, w_pc, b_pc, proj2, b_gate, col_gate, tm, tn):
    n, da = o2.shape
    dc = c2.shape[1]
    d = w_pa.shape[1]
    assert col_gate % tn == 0 and d % tn == 0
    ig = col_gate // tn
    nj = d // tn
    vec = lambda off: pl.BlockSpec((1, tn), lambda i, j: (0, off + j))
    return pl.pallas_call(
        _merge_kernel, grid=(n // tm, nj),
        in_specs=[pl.BlockSpec((tm, da), lambda i, j: (i, 0)), pl.BlockSpec((tm, dc), lambda i, j: (i, 0)),
                  pl.BlockSpec((da, tn), lambda i, j: (0, j)), pl.BlockSpec((dc, tn), lambda i, j: (0, j)), vec(0),
                  pl.BlockSpec((tm, tn), lambda i, j: (i, ig + j)),
                  pl.BlockSpec((tm, tn), lambda i, j: (i, ig + nj + j)),
                  vec(0), vec(nj)],
        out_specs=pl.BlockSpec((tm, tn), lambda i, j: (i, j)),
        out_shape=jax.ShapeDtypeStruct((n, d), BF16),
        compiler_params=_cparams(("parallel", "parallel"), 48), name="merge",
    )(o2, c2, w_pa, w_pc, b_pc.reshape(1, d), proj2, proj2, b_gate.reshape(1, 2 * d), b_gate.reshape(1, 2 * d))


def _outproj_ln_kernel(m_ref, w_ref, b_ref, h_ref, g_ref, be_ref, o_ref, *, alpha):
    mix = jnp.dot(m_ref[...], w_ref[...], preferred_element_type=F32) + b_ref[...]
    o_ref[...] = _ln_rows(alpha * h_ref[...] + mix, g_ref[...], be_ref[...])


def _outproj_ln(merged, w_out, b_out, h, g, b, alpha, tm):
    n, d = h.shape
    row = pl.BlockSpec((tm, d), lambda i: (i, 0))
    vec = pl.BlockSpec((1, d), lambda i: (0, 0))
    return pl.pallas_call(
        functools.partial(_outproj_ln_kernel, alpha=alpha), grid=(n // tm,),
        in_specs=[row, pl.BlockSpec((d, d), lambda i: (0, 0)), vec, row, vec, vec],
        out_specs=row, out_shape=jax.ShapeDtypeStruct((n, d), F32),
        compiler_params=_cparams(("parallel",), 56), name="outproj_ln",
    )(merged, w_out, b_out.reshape(1, d), h, g.reshape(1, d), b.reshape(1, d))


def _router_kernel(h_ref, wr_ref, br_ref, idx_ref, gate_ref, rank_ref, cnt_ref, carry_ref, *, n_exp, t):
    i = pl.program_id(0)

    @pl.when(i == 0)
    def _():
        carry_ref[...] = jnp.zeros(carry_ref.shape, F32)

    logits = lax.dot_general(wr_ref[...], h_ref[...], (((1,), (1,)), ((), ())),
                             precision=lax.Precision.HIGHEST, preferred_element_type=F32) + br_ref[...]
    iota_e = lax.broadcasted_iota(I32, (n_exp, t), 0).astype(F32)
    work = logits
    member = jnp.zeros((n_exp, t), F32)
    vals, idxs = [], []
    for _ in range(TOP_K):
        mx = jnp.max(work, axis=0, keepdims=True)
        idx = jnp.min(jnp.where(work == mx, iota_e, float(n_exp)), axis=0, keepdims=True)
        sel = iota_e == idx
        vals.append(mx)
        idxs.append(idx)
        member = member + sel.astype(F32)
        work = jnp.where(sel, -jnp.inf, work)
    es = [jnp.exp(v - vals[0]) for v in vals]
    inv = 1.0 / functools.reduce(lambda a, b: a + b, es)
    earlier = (lax.broadcasted_iota(I32, (t, t), 0) < lax.broadcasted_iota(I32, (t, t), 1)).astype(BF16)
    carry = carry_ref[...]
    excl = jnp.dot(member.astype(BF16), earlier, preferred_element_type=F32) + carry
    for k in range(TOP_K):
        idx_ref[k:k + 1, :] = idxs[k].astype(I32)
        gate_ref[k:k + 1, :] = es[k] * inv
        rank_ref[k:k + 1, :] = jnp.sum(jnp.where(iota_e == idxs[k], excl, 0.0), axis=0, keepdims=True).astype(I32)
    carry_new = carry + jnp.sum(member, axis=1, keepdims=True)
    carry_ref[...] = carry_new
    cnt_ref[...] = jnp.broadcast_to(carry_new, cnt_ref.shape).astype(I32)


def _router(h, w_router, b_router, t):
    n, d = h.shape
    e = w_router.shape[1]
    out_tok = pl.BlockSpec((TOP_K, t), lambda i: (0, i))
    return pl.pallas_call(
        functools.partial(_router_kernel, n_exp=e, t=t), grid=(n // t,),
        in_specs=[pl.BlockSpec((t, d), lambda i: (i, 0)), pl.BlockSpec((e, d), lambda i: (0, 0)),
                  pl.BlockSpec((e, 1), lambda i: (0, 0))],
        out_specs=[out_tok, out_tok, out_tok, pl.BlockSpec((e, 128), lambda i: (0, 0))],
        out_shape=[jax.ShapeDtypeStruct((TOP_K, n), I32), jax.ShapeDtypeStruct((TOP_K, n), F32),
                   jax.ShapeDtypeStruct((TOP_K, n), I32), jax.ShapeDtypeStruct((e, 128), I32)],
        scratch_shapes=[pltpu.VMEM((e, 1), F32)],
        compiler_params=_cparams(("arbitrary",), 48), name="router",
    )(h, w_router.T, b_router.reshape(e, 1))


def _plan_kernel(cnt_ref, idx_ref, rank_ref, pos_ref, be_ref, nu_ref, *, n_exp, tm_log2, tn, nbp):
    tm = 1 << tm_log2
    cnt = cnt_ref[:, 0:n_exp]
    padded = (((cnt + (tm - 1)) >> tm_log2) << tm_log2).astype(F32)
    row = lax.broadcasted_iota(I32, (n_exp, n_exp), 0)
    col = lax.broadcasted_iota(I32, (n_exp, n_exp), 1)
    padded_row = jnp.sum(jnp.where(row == col, padded, 0.0), axis=0, keepdims=True)
    pstart = jnp.sum(jnp.where(col < row, padded_row, 0.0), axis=1, keepdims=True)
    pend = pstart + padded[:, 0:1]
    iota_e = lax.broadcasted_iota(I32, (n_exp, tn), 0)
    for k in range(TOP_K):
        hit = iota_e == idx_ref[k:k + 1, :]
        pos_ref[k:k + 1, :] = (jnp.sum(jnp.where(hit, pstart, 0.0), axis=0, keepdims=True).astype(I32)
                               + rank_ref[k:k + 1, :])
    blk_start = (lax.broadcasted_iota(I32, (n_exp, nbp), 1) * tm).astype(F32)
    be = jnp.sum((pend <= blk_start).astype(F32), axis=0, keepdims=True)
    be_ref[...] = jnp.minimum(be, n_exp - 1.0).astype(I32)
    total = jnp.max(pend, axis=0, keepdims=True)
    nu_ref[...] = jnp.broadcast_to(total, nu_ref.shape).astype(I32) >> tm_log2


def _plan(cnt, idx, rank, tm, nb, tn):
    e = cnt.shape[0]
    n = idx.shape[1]
    nbp = ((nb + 127) // 128) * 128
    tm_log2 = int(math.log2(tm))
    assert 1 << tm_log2 == tm
    tok = pl.BlockSpec((TOP_K, tn), lambda i: (0, i))
    return pl.pallas_call(
        functools.partial(_plan_kernel, n_exp=e, tm_log2=tm_log2, tn=tn, nbp=nbp), grid=(n // tn,),
        in_specs=[pl.BlockSpec((e, 128), lambda i: (0, 0)), tok, tok],
        out_specs=[tok, pl.BlockSpec((1, nbp), lambda i: (0, 0)), pl.BlockSpec((1, 128), lambda i: (0, 0))],
        out_shape=[jax.ShapeDtypeStruct((TOP_K, n), I32), jax.ShapeDtypeStruct((1, nbp), I32),
                   jax.ShapeDtypeStruct((1, 128), I32)],
        compiler_params=_cparams(("arbitrary",), 48), name="plan",
    )(cnt, idx, rank)


def _dispatch_kernel(pos_ref, h_ref, xs_in_ref, xs_ref, sem, *, t, n):
    del xs_in_ref
    base = pl.program_id(0) * t
    for k in range(TOP_K):
        def body(j, carry, k=k):
            r = pos_ref[k * n + base + j]
            pltpu.make_async_copy(h_ref.at[pl.ds(j, 1)], xs_ref.at[pl.ds(r, 1)], sem).start()
            return carry
        lax.fori_loop(0, t, body, 0, unroll=8)
    for k in range(TOP_K):
        pltpu.make_async_copy(h_ref, xs_ref.at[pl.ds(0, t)], sem).wait()


def _dispatch(pos_flat, h, n_rows, t):
    n, d = h.shape
    grid_spec = pltpu.PrefetchScalarGridSpec(
        num_scalar_prefetch=1, grid=(n // t,),
        in_specs=[pl.BlockSpec((t, d), lambda i, p: (i, 0)), pl.BlockSpec(memory_space=pl.ANY)],
        out_specs=pl.BlockSpec(memory_space=pl.ANY),
        scratch_shapes=[pltpu.SemaphoreType.DMA(())],
    )
    return pl.pallas_call(
        functools.partial(_dispatch_kernel, t=t, n=n), grid_spec=grid_spec,
        out_shape=jax.ShapeDtypeStruct((n_rows, d), h.dtype),
        input_output_aliases={2: 0},
        compiler_params=_cparams(("arbitrary",), 48), name="dispatch",
    )(pos_flat, h, jnp.zeros((n_rows, d), h.dtype))


def _used_block(i, nu_ref):
    return jnp.minimum(i, nu_ref[0] - 1)


def _expert_changed(i, be_ref, nu_ref):
    cur = be_ref[_used_block(i, nu_ref)]
    prev = be_ref[_used_block(jnp.maximum(i - 1, 0), nu_ref)]
    return jnp.logical_or(i == 0, cur != prev)


def _next_run_start(i, e, be_ref, nu_ref):
    def same_expert(i2):
        return jnp.logical_and(i2 < nu_ref[0], be_ref[_used_block(i2, nu_ref)] == e)
    return lax.while_loop(same_expert, lambda i2: i2 + 1, i + 1)


def _gu_kernel(be_ref, nu_ref, x_ref, w_hbm, bg_ref, bu_ref, o_ref, sg_ref, su_ref, wgb_ref, wub_ref, sem,
               *, layer, f, tn):
    j = pl.program_id(0)
    i = pl.program_id(1)
    nj = pl.num_programs(0)

    def copies(e, jj):
        cg = pltpu.make_async_copy(w_hbm.at[layer, e, :, pl.ds(pl.multiple_of(jj * tn, tn), tn)], sg_ref, sem.at[0])
        cu = pltpu.make_async_copy(w_hbm.at[layer, e, :, pl.ds(pl.multiple_of(f + jj * tn, tn), tn)], su_ref,
                                   sem.at[1])
        return cg, cu

    @pl.when(_expert_changed(i, be_ref, nu_ref))
    def _():
        e = be_ref[_used_block(i, nu_ref)]

        @pl.when(jnp.logical_and(j == 0, i == 0))
        def _():
            for cp in copies(e, j):
                cp.start()

        for cp in copies(e, j):
            cp.wait()
        wgb_ref[...] = sg_ref[...].astype(BF16)
        wub_ref[...] = su_ref[...].astype(BF16)
        i2 = _next_run_start(i, e, be_ref, nu_ref)
        more = i2 < nu_ref[0]
        e2 = jnp.where(more, be_ref[_used_block(i2, nu_ref)], be_ref[0])
        j2 = jnp.where(more, j, j + 1)

        @pl.when(j2 < nj)
        def _():
            for cp in copies(e2, j2):
                cp.start()

    @pl.when(i < nu_ref[0])
    def _():
        x = x_ref[...].astype(BF16)
        g = jnp.dot(x, wgb_ref[...], preferred_element_type=F32) + bg_ref[...]
        u = jnp.dot(x, wub_ref[...], preferred_element_type=F32) + bu_ref[...]
        g = jnp.minimum(g, SWIGLU_LIMIT)
        u = jnp.clip(u, -SWIGLU_LIMIT, SWIGLU_LIMIT)
        o_ref[...] = (g * jax.nn.sigmoid(SWIGLU_ALPHA * g) * (u + 1.0)).astype(o_ref.dtype)

    @pl.when(i >= nu_ref[0])
    def _():
        o_ref[...] = jnp.zeros(o_ref.shape, o_ref.dtype)


def _expert_gate_up(be, nu, x, w_gu, b_gu, layer, tm, tn):
    n_rows, kd = x.shape
    nl, e, _, f2 = w_gu.shape
    f = f2 // 2
    nj = f // tn
    row = lambda j, i, be_r, nu_r: (_used_block(i, nu_r), 0)
    bmap = lambda off: (lambda j, i, be_r, nu_r: (layer, be_r[_used_block(i, nu_r)], 0, off + j))
    grid_spec = pltpu.PrefetchScalarGridSpec(
        num_scalar_prefetch=2, grid=(nj, n_rows // tm),
        in_specs=[pl.BlockSpec((tm, kd), row), pl.BlockSpec(memory_space=pl.ANY),
                  pl.BlockSpec((None, None, 1, tn), bmap(0)), pl.BlockSpec((None, None, 1, tn), bmap(nj))],
        out_specs=pl.BlockSpec((tm, tn), lambda j, i, be_r, nu_r: (i, j)),
        scratch_shapes=[pltpu.VMEM((kd, tn), F32), pltpu.VMEM((kd, tn), F32),
                        pltpu.VMEM((kd, tn), BF16), pltpu.VMEM((kd, tn), BF16), pltpu.SemaphoreType.DMA((2,))],
    )
    b4 = b_gu.reshape(nl, e, 1, f2)
    return pl.pallas_call(
        functools.partial(_gu_kernel, layer=layer, f=f, tn=tn), grid_spec=grid_spec,
        out_shape=jax.ShapeDtypeStruct((n_rows, f), BF16),
        compiler_params=_cparams(("arbitrary", "arbitrary"), 48), name="expert_gate_up",
    )(be, nu, x, w_gu, b4, b4)


def _dn_kernel(be_ref, nu_ref, x_ref, w_hbm, b_ref, o_ref, sw_ref, wb_ref, sem, *, layer):
    i = pl.program_id(0)

    def copy(e):
        return pltpu.make_async_copy(w_hbm.at[layer, e], sw_ref, sem)

    @pl.when(_expert_changed(i, be_ref, nu_ref))
    def _():
        e = be_ref[_used_block(i, nu_ref)]

        @pl.when(i == 0)
        def _():
            copy(e).start()

        copy(e).wait()
        wb_ref[...] = sw_ref[...].astype(BF16)
        i2 = _next_run_start(i, e, be_ref, nu_ref)

        @pl.when(i2 < nu_ref[0])
        def _():
            copy(be_ref[_used_block(i2, nu_ref)]).start()

    @pl.when(i < nu_ref[0])
    def _():
        o_ref[...] = jnp.dot(x_ref[...], wb_ref[...], preferred_element_type=F32) + b_ref[...]

    @pl.when(i >= nu_ref[0])
    def _():
        o_ref[...] = jnp.zeros(o_ref.shape, o_ref.dtype)


def _expert_down(be, nu, hid, w_dn, b_dn, layer, tm):
    n_rows, f = hid.shape
    nl, e, _, d = w_dn.shape
    bmap = lambda i, be_r, nu_r: (layer, be_r[_used_block(i, nu_r)], 0, 0)
    grid_spec = pltpu.PrefetchScalarGridSpec(
        num_scalar_prefetch=2, grid=(n_rows // tm,),
        in_specs=[pl.BlockSpec((tm, f), lambda i, be_r, nu_r: (_used_block(i, nu_r), 0)),
                  pl.BlockSpec(memory_space=pl.ANY), pl.BlockSpec((None, None, 1, d), bmap)],
        out_specs=pl.BlockSpec((tm, d), lambda i, be_r, nu_r: (i, 0)),
        scratch_shapes=[pltpu.VMEM((f, d), F32), pltpu.VMEM((f, d), BF16), pltpu.SemaphoreType.DMA(())],
    )
    return pl.pallas_call(
        functools.partial(_dn_kernel, layer=layer), grid_spec=grid_spec,
        out_shape=jax.ShapeDtypeStruct((n_rows, d), F32),
        compiler_params=_cparams(("arbitrary",), 48), name="expert_down",
    )(be, nu, hid, w_dn, b_dn.reshape(nl, e, 1, d))


def _combine_kernel(pos_ref, y_ref, h_ref, gt_ref, g_ref, b_ref, o_ref, ob_ref, ybuf, sem, *, t, n, alpha):
    i = pl.program_id(0)
    nsteps = pl.num_programs(0)

    def issue(step, slot):
        base = step * t
        for k in range(TOP_K):
            def body(j, carry, k=k):
                r = pos_ref[k * n + base + j]
                pltpu.make_async_copy(y_ref.at[pl.ds(r, 1)], ybuf.at[slot, k, pl.ds(j, 1)], sem.at[slot]).start()
                return carry
            lax.fori_loop(0, t, body, 0, unroll=8)

    @pl.when(i == 0)
    def _():
        issue(0, 0)

    slot = i % 2

    @pl.when(i + 1 < nsteps)
    def _():
        issue(i + 1, 1 - slot)

    for k in range(TOP_K):
        pltpu.make_async_copy(y_ref.at[pl.ds(0, t)], ybuf.at[slot, k], sem.at[slot]).wait()
    gt = gt_ref[...]
    ff = gt[:, 0:1] * ybuf[slot, 0]
    for k in range(1, TOP_K):
        ff = ff + gt[:, k:k + 1] * ybuf[slot, k]
    out = _ln_rows(alpha * h_ref[...] + ff, g_ref[...], b_ref[...])
    o_ref[...] = out
    ob_ref[...] = out.astype(BF16)


def _combine(pos_flat, y, h, gates_t, g, b, alpha, t):
    n, d = h.shape
    row = pl.BlockSpec((t, d), lambda i, p: (i, 0))
    vec = pl.BlockSpec((1, d), lambda i, p: (0, 0))
    grid_spec = pltpu.PrefetchScalarGridSpec(
        num_scalar_prefetch=1, grid=(n // t,),
        in_specs=[pl.BlockSpec(memory_space=pl.ANY), row, pl.BlockSpec((t, TOP_K), lambda i, p: (i, 0)), vec, vec],
        out_specs=[row, row],
        scratch_shapes=[pltpu.VMEM((2, TOP_K, t, d), F32), pltpu.SemaphoreType.DMA((2,))],
    )
    return pl.pallas_call(
        functools.partial(_combine_kernel, t=t, n=n, alpha=alpha), grid_spec=grid_spec,
        out_shape=[jax.ShapeDtypeStruct((n, d), F32), jax.ShapeDtypeStruct((n, d), BF16)],
        compiler_params=_cparams(("arbitrary",), 48), name="combine",
    )(pos_flat, y, h, gates_t, g.reshape(1, d), b.reshape(1, d))


def kernel(x, ln_in_g, ln_in_b, w_in, b_gate, lambda_q1, lambda_k1, lambda_q2, lambda_k2, subln_g, w_dw, b_dw,
           conv_ln_g, conv_ln_b, w_pa, w_pc, b_pc, w_out, b_out, ln1_g, ln1_b, w_router, b_router, w_gu, b_gu,
           w_dn, b_dn, ln2_g, ln2_b):
    bsz, seq, d = x.shape
    depth = w_in.shape[0]
    d_att = w_pa.shape[1]
    d_conv = w_pc.shape[1]
    n_heads = d_att // ATT_V_DIM
    n_exp = w_router.shape[-1]
    d_ff = w_dn.shape[-2]
    n = bsz * seq
    alpha = (2 * depth) ** 0.25

    tm_rows = min(512, n)
    t_att = min(512, seq)
    ts_conv = min(512, seq)
    tm_moe = min(256, n)
    t_router = min(512, n)
    t_disp = min(256, n)
    t_comb = min(128, n)
    n_rows = n * TOP_K + n_exp * tm_moe
    nb = n_rows // tm_moe

    h, hb = _layer_norm_in(x.reshape(n, d), ln_in_g, ln_in_b, tm_rows)
    for l in range(depth):
        lam_init = 0.8 - 0.6 * math.exp(-0.3 * l)
        w_in_b = w_in[l].astype(BF16)
        d_in = w_in_b.shape[1]
        proj = _matmul(hb, w_in_b, _tile(n, 1024), _tile(d_in, 1024), BF16, "in_proj")
        proj3 = proj.reshape(bsz, seq, d_in)
        lamv = jnp.stack([lambda_q1[l], lambda_k1[l], lambda_q2[l], lambda_k2[l]]).astype(F32)
        o = _attention(proj3, lamv, subln_g[l], n_heads, lam_init, t_att)
        c = _conv_branch(proj3, w_dw[l].reshape(w_dw.shape[1], d_conv), b_dw[l], conv_ln_g[l], conv_ln_b[l],
                         3 * d_att, ts_conv)
        merged = _merge(o.reshape(n, d_att), c.reshape(n, d_conv), w_pa[l].astype(BF16), w_pc[l].astype(BF16),
                        b_pc[l], proj, b_gate[l], 3 * d_att + 2 * d_conv, tm_rows, min(1024, d))
        h1 = _outproj_ln(merged, w_out[l].astype(BF16), b_out[l], h, ln1_g[l], ln1_b[l], alpha, min(256, n))

        idx, gates, rank, cnt = _router(h1, w_router[l], b_router[l], t_router)
        pos, be, nu = _plan(cnt, idx, rank, tm_moe, nb, min(2048, n))
        pos_flat = pos.reshape(TOP_K * n)
        be_flat = be.reshape(-1)
        nu_flat = nu.reshape(-1)[:1]
        xs = _dispatch(pos_flat, h1, n_rows, t_disp)
        hid = _expert_gate_up(be_flat, nu_flat, xs, w_gu, b_gu, l, tm_moe, _tile(d_ff, 1024))
        y = _expert_down(be_flat, nu_flat, hid, w_dn, b_dn, l, tm_moe)
        h, hb = _combine(pos_flat, y, h1, gates.T, ln2_g[l], ln2_b[l], alpha, t_comb)
    return h.reshape(bsz, seq, d)
```

```python
import functools
import math

import numpy as np
import jax
import jax.numpy as jnp
from jax import lax
from jax.experimental import pallas as pl
from jax.experimental.pallas import tpu as pltpu

F32 = jnp.float32
BF16 = jnp.bfloat16
I32 = jnp.int32

LN_EPS = 1e-5
ATT_HALF_DIM = 64
ATT_V_DIM = 2 * ATT_HALF_DIM
TOP_K = 4
SWIGLU_LIMIT = 7.0
SWIGLU_ALPHA = 1.702
LOG2E = 1.4426950408889634
CONV_HALO = 16
MIB = 1024 * 1024
NEG_BIG = -1e30


def _cparams(semantics, vmem_mib):
    return pltpu.CompilerParams(dimension_semantics=semantics, vmem_limit_bytes=vmem_mib * MIB)


def _tile(total, target, quantum=128):
    if total <= target:
        return total
    t = (target // quantum) * quantum
    while total % t:
        t -= quantum
    return t


def _ln_rows(x, g, b):
    mu = jnp.mean(x, axis=-1, keepdims=True)
    xc = x - mu
    var = jnp.mean(xc * xc, axis=-1, keepdims=True)
    return xc * lax.rsqrt(var + LN_EPS) * g + b


def _ln_kernel(x_ref, g_ref, b_ref, o_ref, ob_ref):
    y = _ln_rows(x_ref[...], g_ref[...], b_ref[...])
    o_ref[...] = y
    ob_ref[...] = y.astype(BF16)


def _layer_norm_in(x2, g, b, tm):
    n, d = x2.shape
    row = pl.BlockSpec((tm, d), lambda i: (i, 0))
    vec = pl.BlockSpec((1, d), lambda i: (0, 0))
    return pl.pallas_call(
        _ln_kernel, grid=(n // tm,), in_specs=[row, vec, vec], out_specs=[row, row],
        out_shape=[jax.ShapeDtypeStruct((n, d), F32), jax.ShapeDtypeStruct((n, d), BF16)],
        compiler_params=_cparams(("parallel",), 48), name="ln_in",
    )(x2, g.reshape(1, d), b.reshape(1, d))


def _mm_kernel(x_ref, w_ref, o_ref):
    o_ref[...] = jnp.dot(x_ref[...], w_ref[...], preferred_element_type=F32).astype(o_ref.dtype)


def _matmul(x, w, tm, tn, out_dtype, name):
    n, kd = x.shape
    m = w.shape[1]
    return pl.pallas_call(
        _mm_kernel, grid=(n // tm, m // tn),
        in_specs=[pl.BlockSpec((tm, kd), lambda i, j: (i, 0)), pl.BlockSpec((kd, tn), lambda i, j: (0, j))],
        out_specs=pl.BlockSpec((tm, tn), lambda i, j: (i, j)),
        out_shape=jax.ShapeDtypeStruct((n, m), out_dtype),
        compiler_params=_cparams(("parallel", "parallel"), 48), name=name,
    )(x, w)


def _attn_kernel(slopes_ref, q_ref, k_ref, v_ref, lamv_ref, g_ref, o_ref,
                 vt_ref, qz_ref, r_ref, sa_ref, sb_ref, p_ref, m_ref, acc_ref, *, t, nk, lam_init, scale):
    h = pl.program_id(1)
    iq = pl.program_id(2)
    slope2 = slopes_ref[h] * LOG2E

    @pl.when(iq == 0)
    def _():
        for j in range(nk):
            vt_ref[j, 0:ATT_V_DIM, :] = v_ref[0, j * t:(j + 1) * t, :].astype(F32).T.astype(BF16)
            ones_row = lax.broadcasted_iota(I32, (16, t), 0) == 0
            vt_ref[j, ATT_V_DIM:ATT_V_DIM + 16, :] = jnp.where(ones_row, 1.0, 0.0).astype(BF16)
        qi = lax.broadcasted_iota(I32, (t, t), 1)
        kj = lax.broadcasted_iota(I32, (t, t), 0)
        r = (qi - kj).astype(F32) * slope2
        r_ref[0] = -r
        r_ref[1] = -jnp.abs(r)
        r_ref[2] = r

    qt = (q_ref[0].astype(F32) * (scale * LOG2E)).T
    row = lax.broadcasted_iota(I32, qt.shape, 0)
    qz_ref[0] = jnp.where(row < ATT_HALF_DIM, qt, 0.0).astype(BF16)
    qz_ref[1] = jnp.where(row >= ATT_HALF_DIM, qt, 0.0).astype(BF16)
    m_ref[...] = jnp.full(m_ref.shape, NEG_BIG, F32)
    acc_ref[...] = jnp.zeros(acc_ref.shape, F32)

    def tile_mode(jk):
        return jnp.where(jk < iq, 0, jnp.where(jk == iq, 1, 2))

    def scores(jk, s_ref):
        kt = k_ref[0, pl.ds(pl.multiple_of(jk * t, t), t), :]
        r = r_ref[tile_mode(jk)]
        for mp in (0, 1):
            s_ref[mp] = jnp.dot(kt, qz_ref[mp], preferred_element_type=F32) + r

    def update(jk, s_ref):
        c = -(slope2 * t) * jnp.abs(iq - jk).astype(F32)
        vt = vt_ref[jk]
        for mp in (0, 1):
            s = s_ref[mp]
            m_old = m_ref[mp]
            m_new = jnp.maximum(m_old, jnp.max(s, axis=0, keepdims=True) + c)
            a = jnp.exp2(m_old - m_new)
            p = jnp.exp2(s - (m_new - c))
            m_ref[mp] = m_new
            p_ref[...] = p.astype(BF16)
            acc_ref[mp] = a * acc_ref[mp] + jnp.dot(vt, p_ref[...], preferred_element_type=F32)

    scores(0, sa_ref)
    bufs = (sa_ref, sb_ref)
    for jk in range(nk):
        if jk + 1 < nk:
            scores(jk + 1, bufs[(jk + 1) % 2])
        update(jk, bufs[jk % 2])

    lv = lamv_ref[...]
    s1 = jnp.sum(lv[0:1] * lv[1:2], axis=-1, keepdims=True)
    s2 = jnp.sum(lv[2:3] * lv[3:4], axis=-1, keepdims=True)
    lam = jnp.exp(s1) - jnp.exp(s2) + lam_init
    vd = ATT_V_DIM
    o = (acc_ref[0, 0:vd, :] * (1.0 / acc_ref[0, vd:vd + 1, :])
         - lam * (acc_ref[1, 0:vd, :] * (1.0 / acc_ref[1, vd:vd + 1, :])))
    ms = jnp.mean(o * o, axis=0, keepdims=True)
    on = o * lax.rsqrt(ms + LN_EPS) * g_ref[...] * (1.0 - lam_init)
    o_ref[0] = on.T.astype(o_ref.dtype)


def _attention(proj3, lamv, subln_g, n_heads, lam_init, t):
    b, s, _ = proj3.shape
    nk = s // t
    vd = ATT_V_DIM
    slopes = jnp.asarray(2.0 ** (-8.0 * np.arange(1, n_heads + 1) / n_heads), dtype=F32)
    kern = functools.partial(_attn_kernel, t=t, nk=nk, lam_init=lam_init, scale=ATT_HALF_DIM ** -0.5)
    grid_spec = pltpu.PrefetchScalarGridSpec(
        num_scalar_prefetch=1, grid=(b, n_heads, nk),
        in_specs=[
            pl.BlockSpec((1, t, vd), lambda bi, hi, qi, sl: (bi, qi, hi)),
            pl.BlockSpec((1, s, vd), lambda bi, hi, qi, sl: (bi, 0, n_heads + hi)),
            pl.BlockSpec((1, s, vd), lambda bi, hi, qi, sl: (bi, 0, 2 * n_heads + hi)),
            pl.BlockSpec((4, ATT_HALF_DIM), lambda bi, hi, qi, sl: (0, 0)),
            pl.BlockSpec((vd, 1), lambda bi, hi, qi, sl: (0, 0)),
        ],
        out_specs=pl.BlockSpec((1, t, vd), lambda bi, hi, qi, sl: (bi, qi, hi)),
        scratch_shapes=[
            pltpu.VMEM((nk, vd + 16, t), BF16),
            pltpu.VMEM((2, vd, t), BF16),
            pltpu.VMEM((3, t, t), F32),
            pltpu.VMEM((2, t, t), F32),
            pltpu.VMEM((2, t, t), F32),
            pltpu.VMEM((t, t), BF16),
            pltpu.VMEM((2, 1, t), F32),
            pltpu.VMEM((2, vd + 16, t), F32),
        ],
    )
    return pl.pallas_call(
        kern, grid_spec=grid_spec,
        out_shape=jax.ShapeDtypeStruct((b, s, n_heads * vd), BF16),
        compiler_params=_cparams(("parallel", "parallel", "arbitrary"), 48), name="diff_attn",
    )(slopes, proj3, proj3, proj3, lamv, subln_g.reshape(vd, 1))


def _conv_kernel(vc_ref, gc_ref, vp_ref, gp_ref, vn_ref, gn_ref, w_ref, b_ref, lg_ref, lb_ref,
                 o_ref, ext_ref, sh_ref, *, ts, width, rc):
    i = pl.program_id(1)
    n = pl.num_programs(1)

    def glu(v_ref, g_ref):
        return v_ref[0].astype(F32) * jax.nn.sigmoid(g_ref[0].astype(F32))

    hp = CONV_HALO
    ext_ref[0:hp, :] = glu(vp_ref, gp_ref) * (i > 0).astype(F32)
    ext_ref[hp:hp + ts, :] = glu(vc_ref, gc_ref)
    ext_ref[hp + ts:hp + ts + hp, :] = glu(vn_ref, gn_ref) * (i < n - 1).astype(F32)
    rows = sh_ref.shape[1]
    for s in range(8):
        sh_ref[s] = ext_ref[s:s + rows, :]
    pad = width // 2
    c = o_ref.shape[-1]
    for r0 in range(0, ts, rc):
        acc = jnp.zeros((rc, c), F32) + b_ref[...]
        for j in range(width):
            off = r0 + hp - pad + j
            acc = acc + sh_ref[off % 8, off - off % 8:off - off % 8 + rc, :] * w_ref[j:j + 1, :]
        y = _ln_rows(acc, lg_ref[...], lb_ref[...])
        o_ref[0, r0:r0 + rc, :] = (y * jax.nn.sigmoid(y)).astype(o_ref.dtype)


def _conv_branch(proj3, w_dw, b_dw, g_ln, b_ln, col_val, ts):
    b, s, _ = proj3.shape
    width, c = w_dw.shape
    assert width // 2 <= CONV_HALO and col_val % c == 0 and ts % CONV_HALO == 0
    iv = col_val // c
    hb = ts // CONV_HALO
    nhb = s // CONV_HALO
    cur = lambda off: pl.BlockSpec((1, ts, c), lambda bi, i: (bi, i, iv + off))
    prev = lambda off: pl.BlockSpec((1, CONV_HALO, c), lambda bi, i: (bi, jnp.maximum(i * hb - 1, 0), iv + off))
    nxt = lambda off: pl.BlockSpec((1, CONV_HALO, c), lambda bi, i: (bi, jnp.minimum((i + 1) * hb, nhb - 1), iv + off))
    vec = pl.BlockSpec((1, c), lambda bi, i: (0, 0))
    kern = functools.partial(_conv_kernel, ts=ts, width=width, rc=32)
    return pl.pallas_call(
        kern, grid=(b, s // ts),
        in_specs=[cur(0), cur(1), prev(0), prev(1), nxt(0), nxt(1),
                  pl.BlockSpec((width, c), lambda bi, i: (0, 0)), vec, vec, vec],
        out_specs=pl.BlockSpec((1, ts, c), lambda bi, i: (bi, i, 0)),
        out_shape=jax.ShapeDtypeStruct((b, s, c), BF16),
        scratch_shapes=[pltpu.VMEM((ts + 2 * CONV_HALO, c), F32), pltpu.VMEM((8, ts + CONV_HALO + 8, c), F32)],
        compiler_params=_cparams(("parallel", "parallel"), 48), name="conv_branch",
    )(proj3, proj3, proj3, proj3, proj3, proj3, w_dw, b_dw.reshape(1, c), g_ln.reshape(1, c), b_ln.reshape(1, c))


def _merge_kernel(o_ref, c_ref, wpa_ref, wpc_ref, bpc_ref, ga_ref, gc_ref, bga_ref, bgc_ref, out_ref):
    ya = jnp.dot(o_ref[...], wpa_ref[...], preferred_element_type=F32)
    yc = jnp.dot(c_ref[...], wpc_ref[...], preferred_element_type=F32) + bpc_ref[...]
    g_att = jax.nn.sigmoid(ga_ref[...].astype(F32) + bga_ref[...])
    g_conv = jax.nn.sigmoid(gc_ref[...].astype(F32) + bgc_ref[...])
    out_ref[...] = (g_att * ya + g_conv * yc).astype(out_ref.dtype)


def _merge(o2, c2, w_pa, w_pc, b_pc, proj2, b_gate, col_gate, tm, tn):
    n, da = o2.shape
    dc = c2.shape[1]
    d = w_pa.shape[1]
    assert col_gate % tn == 0 and d % tn == 0
    ig = col_gate // tn
    nj = d // tn
    vec = lambda off: pl.BlockSpec((1, tn), lambda i, j: (0, off + j))
    return pl.pallas_call(
        _merge_kernel, grid=(n // tm, nj),
        in_specs=[pl.BlockSpec((tm, da), lambda i, j: (i, 0)), pl.BlockSpec((tm, dc), lambda i, j: (i, 0)),
                  pl.BlockSpec((da, tn), lambda i, j: (0, j)), pl.BlockSpec((dc, tn), lambda i, j: (0, j)), vec(0),
                  pl.BlockSpec((tm, tn), lambda i, j: (i, ig + j)),
                  pl.BlockSpec((tm, tn), lambda i, j: (i, ig + nj + j)),
                  vec(0), vec(nj)],
        out_specs=pl.BlockSpec((tm, tn), lambda i, j: (i, j)),
        out_shape=jax.ShapeDtypeStruct((n, d), BF16),
        compiler_params=_cparams(("parallel", "parallel"), 48), name="merge",
    )(o2, c2, w_pa, w_pc, b_pc.reshape(1, d), proj2, proj2, b_gate.reshape(1, 2 * d), b_gate.reshape(1, 2 * d))


def _outproj_ln_kernel(m_ref, w_ref, b_ref, h_ref, g_ref, be_ref, o_ref, *, alpha):
    mix = jnp.dot(m_ref[...], w_ref[...], preferred_element_type=F32) + b_ref[...]
    o_ref[...] = _ln_rows(alpha * h_ref[...] + mix, g_ref[...], be_ref[...])


def _outproj_ln(merged, w_out, b_out, h, g, b, alpha, tm):
    n, d = h.shape
    row = pl.BlockSpec((tm, d), lambda i: (i, 0))
    vec = pl.BlockSpec((1, d), lambda i: (0, 0))
    return pl.pallas_call(
        functools.partial(_outproj_ln_kernel, alpha=alpha), grid=(n // tm,),
        in_specs=[row, pl.BlockSpec((d, d), lambda i: (0, 0)), vec, row, vec, vec],
        out_specs=row, out_shape=jax.ShapeDtypeStruct((n, d), F32),
        compiler_params=_cparams(("parallel",), 56), name="outproj_ln",
    )(merged, w_out, b_out.reshape(1, d), h, g.reshape(1, d), b.reshape(1, d))


def _router_kernel(h_ref, wr_ref, br_ref, idx_ref, gate_ref, rank_ref, cnt_ref, carry_ref, *, n_exp, t):
    i = pl.program_id(0)

    @pl.when(i == 0)
    def _():
        carry_ref[...] = jnp.zeros(carry_ref.shape, F32)

    logits = lax.dot_general(wr_ref[...], h_ref[...], (((1,), (1,)), ((), ())),
                             precision=lax.Precision.HIGHEST, preferred_element_type=F32) + br_ref[...]
    iota_e = lax.broadcasted_iota(I32, (n_exp, t), 0).astype(F32)
    work = logits
    member = jnp.zeros((n_exp, t), F32)
    vals, idxs = [], []
    for _ in range(TOP_K):
        mx = jnp.max(work, axis=0, keepdims=True)
        idx = jnp.min(jnp.where(work == mx, iota_e, float(n_exp)), axis=0, keepdims=True)
        sel = iota_e == idx
        vals.append(mx)
        idxs.append(idx)
        member = member + sel.astype(F32)
        work = jnp.where(sel, -jnp.inf, work)
    es = [jnp.exp(v - vals[0]) for v in vals]
    inv = 1.0 / functools.reduce(lambda a, b: a + b, es)
    earlier = (lax.broadcasted_iota(I32, (t, t), 0) < lax.broadcasted_iota(I32, (t, t), 1)).astype(BF16)
    carry = carry_ref[...]
    excl = jnp.dot(member.astype(BF16), earlier, preferred_element_type=F32) + carry
    for k in range(TOP_K):
        idx_ref[k:k + 1, :] = idxs[k].astype(I32)
        gate_ref[k:k + 1, :] = es[k] * inv
        rank_ref[k:k + 1, :] = jnp.sum(jnp.where(iota_e == idxs[k], excl, 0.0), axis=0, keepdims=True).astype(I32)
    carry_new = carry + jnp.sum(member, axis=1, keepdims=True)
    carry_ref[...] = carry_new
    cnt_ref[...] = jnp.broadcast_to(carry_new, cnt_ref.shape).astype(I32)


def _router(h, w_router, b_router, t):
    n, d = h.shape
    e = w_router.shape[1]
    out_tok = pl.BlockSpec((TOP_K, t), lambda i: (0, i))
    return pl.pallas_call(
        functools.partial(_router_kernel, n_exp=e, t=t), grid=(n // t,),
        in_specs=[pl.BlockSpec((t, d), lambda i: (i, 0)), pl.BlockSpec((e, d), lambda i: (0, 0)),
                  pl.BlockSpec((e, 1), lambda i: (0, 0))],
        out_specs=[out_tok, out_tok, out_tok, pl.BlockSpec((e, 128), lambda i: (0, 0))],
        out_shape=[jax.ShapeDtypeStruct((TOP_K, n), I32), jax.ShapeDtypeStruct((TOP_K, n), F32),
                   jax.ShapeDtypeStruct((TOP_K, n), I32), jax.ShapeDtypeStruct((e, 128), I32)],
        scratch_shapes=[pltpu.VMEM((e, 1), F32)],
        compiler_params=_cparams(("arbitrary",), 48), name="router",
    )(h, w_router.T, b_router.reshape(e, 1))


def _plan_kernel(cnt_ref, idx_ref, rank_ref, pos_ref, be_ref, nu_ref, pad_ref, *, n_exp, tm_log2, tn, nbp):
    tm = 1 << tm_log2
    cnt = cnt_ref[:, 0:n_exp]
    padded = (((cnt + (tm - 1)) >> tm_log2) << tm_log2).astype(F32)
    row = lax.broadcasted_iota(I32, (n_exp, n_exp), 0)
    col = lax.broadcasted_iota(I32, (n_exp, n_exp), 1)
    padded_row = jnp.sum(jnp.where(row == col, padded, 0.0), axis=0, keepdims=True)
    pstart = jnp.sum(jnp.where(col < row, padded_row, 0.0), axis=1, keepdims=True)
    pend = pstart + padded[:, 0:1]
    iota_e = lax.broadcasted_iota(I32, (n_exp, tn), 0)
    for k in range(TOP_K):
        hit = iota_e == idx_ref[k:k + 1, :]
        pos_ref[k:k + 1, :] = (jnp.sum(jnp.where(hit, pstart, 0.0), axis=0, keepdims=True).astype(I32)
                               + rank_ref[k:k + 1, :])
    blk_start = (lax.broadcasted_iota(I32, (n_exp, nbp), 1) * tm).astype(F32)
    be = jnp.sum((pend <= blk_start).astype(F32), axis=0, keepdims=True)
    be_ref[...] = jnp.minimum(be, n_exp - 1.0).astype(I32)
    total = jnp.max(pend, axis=0, keepdims=True)
    nu_ref[...] = jnp.broadcast_to(total, nu_ref.shape).astype(I32) >> tm_log2
    cnt_w = cnt_ref[...]
    padded_w = (((cnt_w + (tm - 1)) >> tm_log2) << tm_log2).astype(F32)
    row_w = lax.broadcasted_iota(I32, cnt_w.shape, 0)
    col_w = lax.broadcasted_iota(I32, cnt_w.shape, 1)
    on_diag = row_w == col_w
    pstart_row = jnp.sum(jnp.where(row_w < col_w, padded_w, 0.0), axis=0, keepdims=True)
    cnt_row = jnp.sum(jnp.where(on_diag, cnt_w.astype(F32), 0.0), axis=0, keepdims=True)
    padded_row_w = jnp.sum(jnp.where(on_diag, padded_w, 0.0), axis=0, keepdims=True)
    pad_ref[0:1, :] = (pstart_row + cnt_row).astype(I32)
    pad_ref[1:2, :] = (pstart_row + padded_row_w).astype(I32)


def _plan(cnt, idx, rank, tm, nb, tn):
    e = cnt.shape[0]
    n = idx.shape[1]
    nbp = ((nb + 127) // 128) * 128
    tm_log2 = int(math.log2(tm))
    assert 1 << tm_log2 == tm
    tok = pl.BlockSpec((TOP_K, tn), lambda i: (0, i))
    return pl.pallas_call(
        functools.partial(_plan_kernel, n_exp=e, tm_log2=tm_log2, tn=tn, nbp=nbp), grid=(n // tn,),
        in_specs=[pl.BlockSpec((e, 128), lambda i: (0, 0)), tok, tok],
        out_specs=[tok, pl.BlockSpec((1, nbp), lambda i: (0, 0)), pl.BlockSpec((1, 128), lambda i: (0, 0)),
                   pl.BlockSpec((2, 128), lambda i: (0, 0))],
        out_shape=[jax.ShapeDtypeStruct((TOP_K, n), I32), jax.ShapeDtypeStruct((1, nbp), I32),
                   jax.ShapeDtypeStruct((1, 128), I32), jax.ShapeDtypeStruct((2, 128), I32)],
        compiler_params=_cparams(("arbitrary",), 48), name="plan",
    )(cnt, idx, rank)


def _dispatch_kernel(pos_ref, pad_ref, nu_ref, h_ref, xs_ref, zero_ref, sem, zsem, *, t, n, n_exp, tm, nb):
    base = pl.program_id(0) * t

    @pl.when(pl.program_id(0) == 0)
    def _():
        zero_ref[...] = jnp.zeros(zero_ref.shape, zero_ref.dtype)

        def zero_copy(r):
            return pltpu.make_async_copy(zero_ref.at[pl.ds(0, 1)], xs_ref.at[pl.ds(r, 1)], zsem)

        def zero_block(blk):
            return pltpu.make_async_copy(zero_ref, xs_ref.at[pl.ds(pl.multiple_of(blk * tm, tm), tm)], zsem)

        lax.fori_loop(nu_ref[0], nb, lambda blk, c: (zero_block(blk).start(), c)[1], 0)
        lax.fori_loop(nu_ref[0], nb, lambda blk, c: (zero_block(blk).wait(), c)[1], 0)

        for e in range(n_exp):
            lo, hi = pad_ref[e], pad_ref[128 + e]
            lax.fori_loop(lo, hi, lambda r, c: (zero_copy(r).start(), c)[1], 0)
        for e in range(n_exp):
            lo, hi = pad_ref[e], pad_ref[128 + e]
            lax.fori_loop(lo, hi, lambda r, c: (zero_copy(r).wait(), c)[1], 0)

    for k in range(TOP_K):
        def body(j, carry, k=k):
            r = pos_ref[k * n + base + j]
            pltpu.make_async_copy(h_ref.at[pl.ds(j, 1)], xs_ref.at[pl.ds(r, 1)], sem).start()
            return carry
        lax.fori_loop(0, t, body, 0, unroll=16)
    for k in range(TOP_K):
        pltpu.make_async_copy(h_ref, xs_ref.at[pl.ds(0, t)], sem).wait()


def _dispatch(pos_flat, pad_flat, nu, h, n_rows, n_exp, tm, t):
    n, d = h.shape
    grid_spec = pltpu.PrefetchScalarGridSpec(
        num_scalar_prefetch=3, grid=(n // t,),
        in_specs=[pl.BlockSpec((t, d), lambda i, p, q, u: (i, 0))],
        out_specs=pl.BlockSpec(memory_space=pl.ANY),
        scratch_shapes=[pltpu.VMEM((tm, d), h.dtype), pltpu.SemaphoreType.DMA(()), pltpu.SemaphoreType.DMA(())],
    )
    return pl.pallas_call(
        functools.partial(_dispatch_kernel, t=t, n=n, n_exp=n_exp, tm=tm, nb=n_rows // tm), grid_spec=grid_spec,
        out_shape=jax.ShapeDtypeStruct((n_rows, d), h.dtype),
        compiler_params=_cparams(("arbitrary",), 48), name="dispatch",
    )(pos_flat, pad_flat, nu, h)


def _used_block(i, nu_ref):
    return jnp.minimum(i, nu_ref[0] - 1)


def _expert_changed(i, be_ref, nu_ref):
    cur = be_ref[_used_block(i, nu_ref)]
    prev = be_ref[_used_block(jnp.maximum(i - 1, 0), nu_ref)]
    return jnp.logical_or(i == 0, cur != prev)


def _next_run_start(i, e, be_ref, nu_ref):
    def same_expert(i2):
        return jnp.logical_and(i2 < nu_ref[0], be_ref[_used_block(i2, nu_ref)] == e)
    return lax.while_loop(same_expert, lambda i2: i2 + 1, i + 1)


def _gu_kernel(be_ref, nu_ref, x_ref, w_hbm, bg_ref, bu_ref, o_ref, sg_ref, su_ref, wgb_ref, wub_ref, sem,
               *, layer, f, tn):
    j = pl.program_id(0)
    i = pl.program_id(1)
    nj = pl.num_programs(0)

    def copies(e, jj):
        cg = pltpu.make_async_copy(w_hbm.at[layer, e, :, pl.ds(pl.multiple_of(jj * tn, tn), tn)], sg_ref, sem.at[0])
        cu = pltpu.make_async_copy(w_hbm.at[layer, e, :, pl.ds(pl.multiple_of(f + jj * tn, tn), tn)], su_ref,
                                   sem.at[1])
        return cg, cu

    @pl.when(_expert_changed(i, be_ref, nu_ref))
    def _():
        e = be_ref[_used_block(i, nu_ref)]

        @pl.when(jnp.logical_and(j == 0, i == 0))
        def _():
            for cp in copies(e, j):
                cp.start()

        for cp in copies(e, j):
            cp.wait()
        wgb_ref[...] = sg_ref[...].astype(BF16)
        wub_ref[...] = su_ref[...].astype(BF16)
        i2 = _next_run_start(i, e, be_ref, nu_ref)
        more = i2 < nu_ref[0]
        e2 = jnp.where(more, be_ref[_used_block(i2, nu_ref)], be_ref[0])
        j2 = jnp.where(more, j, j + 1)

        @pl.when(j2 < nj)
        def _():
            for cp in copies(e2, j2):
                cp.start()

    @pl.when(i < nu_ref[0])
    def _():
        x = x_ref[...].astype(BF16)
        g = jnp.dot(x, wgb_ref[...], preferred_element_type=F32) + bg_ref[...]
        u = jnp.dot(x, wub_ref[...], preferred_element_type=F32) + bu_ref[...]
        g = jnp.minimum(g, SWIGLU_LIMIT)
        u = jnp.clip(u, -SWIGLU_LIMIT, SWIGLU_LIMIT)
        o_ref[...] = (g * jax.nn.sigmoid(SWIGLU_ALPHA * g) * (u + 1.0)).astype(o_ref.dtype)

    @pl.when(i >= nu_ref[0])
    def _():
        o_ref[...] = jnp.zeros(o_ref.shape, o_ref.dtype)


def _expert_gate_up(be, nu, x, w_gu, b_gu, layer, tm, tn):
    n_rows, kd = x.shape
    nl, e, _, f2 = w_gu.shape
    f = f2 // 2
    nj = f // tn
    row = lambda j, i, be_r, nu_r: (_used_block(i, nu_r), 0)
    bmap = lambda off: (lambda j, i, be_r, nu_r: (layer, be_r[_used_block(i, nu_r)], 0, off + j))
    grid_spec = pltpu.PrefetchScalarGridSpec(
        num_scalar_prefetch=2, grid=(nj, n_rows // tm),
        in_specs=[pl.BlockSpec((tm, kd), row), pl.BlockSpec(memory_space=pl.ANY),
                  pl.BlockSpec((None, None, 1, tn), bmap(0)), pl.BlockSpec((None, None, 1, tn), bmap(nj))],
        out_specs=pl.BlockSpec((tm, tn), lambda j, i, be_r, nu_r: (i, j)),
        scratch_shapes=[pltpu.VMEM((kd, tn), F32), pltpu.VMEM((kd, tn), F32),
                        pltpu.VMEM((kd, tn), BF16), pltpu.VMEM((kd, tn), BF16), pltpu.SemaphoreType.DMA((2,))],
    )
    b4 = b_gu.reshape(nl, e, 1, f2)
    return pl.pallas_call(
        functools.partial(_gu_kernel, layer=layer, f=f, tn=tn), grid_spec=grid_spec,
        out_shape=jax.ShapeDtypeStruct((n_rows, f), BF16),
        compiler_params=_cparams(("arbitrary", "arbitrary"), 48), name="expert_gate_up",
    )(be, nu, x, w_gu, b4, b4)


def _dn_kernel(be_ref, nu_ref, x_ref, w_hbm, b_ref, o_ref, sw_ref, wb_ref, sem, *, layer):
    i = pl.program_id(0)

    def copy(e):
        return pltpu.make_async_copy(w_hbm.at[layer, e], sw_ref, sem)

    @pl.when(_expert_changed(i, be_ref, nu_ref))
    def _():
        e = be_ref[_used_block(i, nu_ref)]

        @pl.when(i == 0)
        def _():
            copy(e).start()

        copy(e).wait()
        wb_ref[...] = sw_ref[...].astype(BF16)
        i2 = _next_run_start(i, e, be_ref, nu_ref)

        @pl.when(i2 < nu_ref[0])
        def _():
            copy(be_ref[_used_block(i2, nu_ref)]).start()

    @pl.when(i < nu_ref[0])
    def _():
        o_ref[...] = jnp.dot(x_ref[...], wb_ref[...], preferred_element_type=F32) + b_ref[...]

    @pl.when(i >= nu_ref[0])
    def _():
        o_ref[...] = jnp.zeros(o_ref.shape, o_ref.dtype)


def _expert_down(be, nu, hid, w_dn, b_dn, layer, tm):
    n_rows, f = hid.shape
    nl, e, _, d = w_dn.shape
    bmap = lambda i, be_r, nu_r: (layer, be_r[_used_block(i, nu_r)], 0, 0)
    grid_spec = pltpu.PrefetchScalarGridSpec(
        num_scalar_prefetch=2, grid=(n_rows // tm,),
        in_specs=[pl.BlockSpec((tm, f), lambda i, be_r, nu_r: (_used_block(i, nu_r), 0)),
                  pl.BlockSpec(memory_space=pl.ANY), pl.BlockSpec((None, None, 1, d), bmap)],
        out_specs=pl.BlockSpec((tm, d), lambda i, be_r, nu_r: (i, 0)),
        scratch_shapes=[pltpu.VMEM((f, d), F32), pltpu.VMEM((f, d), BF16), pltpu.SemaphoreType.DMA(())],
    )
    return pl.pallas_call(
        functools.partial(_dn_kernel, layer=layer), grid_spec=grid_spec,
        out_shape=jax.ShapeDtypeStruct((n_rows, d), F32),
        compiler_params=_cparams(("arbitrary",), 48), name="expert_down",
    )(be, nu, hid, w_dn, b_dn.reshape(nl, e, 1, d))


def _combine_kernel(pos_ref, y_ref, h_ref, gt_ref, g_ref, b_ref, o_ref, ob_ref, ybuf, sem, *, t, n, alpha):
    i = pl.program_id(0)
    nsteps = pl.num_programs(0)

    def issue(step, slot):
        base = step * t
        for k in range(TOP_K):
            def body(j, carry, k=k):
                r = pos_ref[k * n + base + j]
                pltpu.make_async_copy(y_ref.at[pl.ds(r, 1)], ybuf.at[slot, k, pl.ds(j, 1)], sem.at[slot]).start()
                return carry
            lax.fori_loop(0, t, body, 0, unroll=16)

    @pl.when(i == 0)
    def _():
        issue(0, 0)

    slot = i % 2

    @pl.when(i + 1 < nsteps)
    def _():
        issue(i + 1, 1 - slot)

    for k in range(TOP_K):
        pltpu.make_async_copy(y_ref.at[pl.ds(0, t)], ybuf.at[slot, k], sem.at[slot]).wait()
    gt = gt_ref[...]
    ff = gt[:, 0:1] * ybuf[slot, 0]
    for k in range(1, TOP_K):
        ff = ff + gt[:, k:k + 1] * ybuf[slot, k]
    out = _ln_rows(alpha * h_ref[...] + ff, g_ref[...], b_ref[...])
    o_ref[...] = out
    ob_ref[...] = out.astype(BF16)


def _combine(pos_flat, y, h, gates_t, g, b, alpha, t):
    n, d = h.shape
    row = pl.BlockSpec((t, d), lambda i, p: (i, 0))
    vec = pl.BlockSpec((1, d), lambda i, p: (0, 0))
    grid_spec = pltpu.PrefetchScalarGridSpec(
        num_scalar_prefetch=1, grid=(n // t,),
        in_specs=[pl.BlockSpec(memory_space=pl.ANY), row, pl.BlockSpec((t, TOP_K), lambda i, p: (i, 0)), vec, vec],
        out_specs=[row, row],
        scratch_shapes=[pltpu.VMEM((2, TOP_K, t, d), F32), pltpu.SemaphoreType.DMA((2,))],
    )
    return pl.pallas_call(
        functools.partial(_combine_kernel, t=t, n=n, alpha=alpha), grid_spec=grid_spec,
        out_shape=[jax.ShapeDtypeStruct((n, d), F32), jax.ShapeDtypeStruct((n, d), BF16)],
        compiler_params=_cparams(("arbitrary",), 48), name="combine",
    )(pos_flat, y, h, gates_t, g.reshape(1, d), b.reshape(1, d))


def kernel(x, ln_in_g, ln_in_b, w_in, b_gate, lambda_q1, lambda_k1, lambda_q2, lambda_k2, subln_g, w_dw, b_dw,
           conv_ln_g, conv_ln_b, w_pa, w_pc, b_pc, w_out, b_out, ln1_g, ln1_b, w_router, b_router, w_gu, b_gu,
           w_dn, b_dn, ln2_g, ln2_b):
    bsz, seq, d = x.shape
    depth = w_in.shape[0]
    d_att = w_pa.shape[1]
    d_conv = w_pc.shape[1]
    n_heads = d_att // ATT_V_DIM
    n_exp = w_router.shape[-1]
    d_ff = w_dn.shape[-2]
    n = bsz * seq
    alpha = (2 * depth) ** 0.25

    tm_rows = min(512, n)
    t_att = min(512, seq)
    ts_conv = min(512, seq)
    tm_moe = min(512, n)
    t_router = min(512, n)
    t_disp = min(256, n)
    t_comb = min(128, n)
    n_rows = n * TOP_K + n_exp * tm_moe
    nb = n_rows // tm_moe

    h, hb = _layer_norm_in(x.reshape(n, d), ln_in_g, ln_in_b, tm_rows)
    for l in range(depth):
        lam_init = 0.8 - 0.6 * math.exp(-0.3 * l)
        w_in_b = w_in[l].astype(BF16)
        d_in = w_in_b.shape[1]
        proj = _matmul(hb, w_in_b, _tile(n, 1024), _tile(d_in, 1024), BF16, "in_proj")
        proj3 = proj.reshape(bsz, seq, d_in)
        lamv = jnp.stack([lambda_q1[l], lambda_k1[l], lambda_q2[l], lambda_k2[l]]).astype(F32)
        o = _attention(proj3, lamv, subln_g[l], n_heads, lam_init, t_att)
        c = _conv_branch(proj3, w_dw[l].reshape(w_dw.shape[1], d_conv), b_dw[l], conv_ln_g[l], conv_ln_b[l],
                         3 * d_att, ts_conv)
        merged = _merge(o.reshape(n, d_att), c.reshape(n, d_conv), w_pa[l].astype(BF16), w_pc[l].astype(BF16),
                        b_pc[l], proj, b_gate[l], 3 * d_att + 2 * d_conv, tm_rows, min(1024, d))
        h1 = _outproj_ln(merged, w_out[l].astype(BF16), b_out[l], h, ln1_g[l], ln1_b[l], alpha, min(256, n))

        idx, gates, rank, cnt = _router(h1, w_router[l], b_router[l], t_router)
        pos, be, nu, pad = _plan(cnt, idx, rank, tm_moe, nb, min(2048, n))
        pos_flat = pos.reshape(TOP_K * n)
        be_flat = be.reshape(-1)
        nu_flat = nu.reshape(-1)[:1]
        xs = _dispatch(pos_flat, pad.reshape(-1), nu_flat, h1, n_rows, n_exp, tm_moe, t_disp)
        hid = _expert_gate_up(be_flat, nu_flat, xs, w_gu, b_gu, l, tm_moe, _tile(d_ff, 1024))
        y = _expert_down(be_flat, nu_flat, hid, w_dn, b_dn, l, tm_moe)
        h, hb = _combine(pos_flat, y, h1, gates.T, ln2_g[l], ln2_b[l], alpha, t_comb)
    return h.reshape(bsz, seq, d)
```

```python
import functools
import math

import numpy as np
import jax
import jax.numpy as jnp
from jax import lax
from jax.experimental import pallas as pl
from jax.experimental.pallas import tpu as pltpu

F32 = jnp.float32
BF16 = jnp.bfloat16
I32 = jnp.int32

LN_EPS = 1e-5
ATT_HALF_DIM = 64
ATT_V_DIM = 2 * ATT_HALF_DIM
TOP_K = 4
SWIGLU_LIMIT = 7.0
SWIGLU_ALPHA = 1.702
LOG2E = 1.4426950408889634
CONV_HALO = 16
MIB = 1024 * 1024
NEG_BIG = -1e30


def _cparams(semantics, vmem_mib):
    return pltpu.CompilerParams(dimension_semantics=semantics, vmem_limit_bytes=vmem_mib * MIB)


def _tile(total, target, quantum=128):
    if total <= target:
        return total
    t = (target // quantum) * quantum
    while total % t:
        t -= quantum
    return t


def _ln_rows(x, g, b):
    mu = jnp.mean(x, axis=-1, keepdims=True)
    xc = x - mu
    var = jnp.mean(xc * xc, axis=-1, keepdims=True)
    return xc * lax.rsqrt(var + LN_EPS) * g + b


def _ln_kernel(x_ref, g_ref, b_ref, o_ref, ob_ref):
    y = _ln_rows(x_ref[...], g_ref[...], b_ref[...])
    o_ref[...] = y
    ob_ref[...] = y.astype(BF16)


def _layer_norm_in(x2, g, b, tm):
    n, d = x2.shape
    row = pl.BlockSpec((tm, d), lambda i: (i, 0))
    vec = pl.BlockSpec((1, d), lambda i: (0, 0))
    return pl.pallas_call(
        _ln_kernel, grid=(n // tm,), in_specs=[row, vec, vec], out_specs=[row, row],
        out_shape=[jax.ShapeDtypeStruct((n, d), F32), jax.ShapeDtypeStruct((n, d), BF16)],
        compiler_params=_cparams(("parallel",), 48), name="ln_in",
    )(x2, g.reshape(1, d), b.reshape(1, d))


def _mm_kernel(x_ref, w_ref, o_ref):
    o_ref[...] = jnp.dot(x_ref[...], w_ref[...], preferred_element_type=F32).astype(o_ref.dtype)


def _matmul(x, w, tm, tn, out_dtype, name):
    n, kd = x.shape
    m = w.shape[1]
    return pl.pallas_call(
        _mm_kernel, grid=(n // tm, m // tn),
        in_specs=[pl.BlockSpec((tm, kd), lambda i, j: (i, 0)), pl.BlockSpec((kd, tn), lambda i, j: (0, j))],
        out_specs=pl.BlockSpec((tm, tn), lambda i, j: (i, j)),
        out_shape=jax.ShapeDtypeStruct((n, m), out_dtype),
        compiler_params=_cparams(("parallel", "parallel"), 48), name=name,
    )(x, w)


def _attn_kernel(slopes_ref, q_ref, k_ref, v_ref, lamv_ref, g_ref, o_ref,
                 vt_ref, qz_ref, r_ref, sa_ref, sb_ref, p_ref, m_ref, acc_ref, *, t, nk, lam_init, scale):
    h = pl.program_id(1)
    iq = pl.program_id(2)
    slope2 = slopes_ref[h] * LOG2E

    @pl.when(iq == 0)
    def _():
        for j in range(nk):
            vt_ref[j, 0:ATT_V_DIM, :] = v_ref[0, j * t:(j + 1) * t, :].astype(F32).T.astype(BF16)
            ones_row = lax.broadcasted_iota(I32, (16, t), 0) == 0
            vt_ref[j, ATT_V_DIM:ATT_V_DIM + 16, :] = jnp.where(ones_row, 1.0, 0.0).astype(BF16)
        qi = lax.broadcasted_iota(I32, (t, t), 1)
        kj = lax.broadcasted_iota(I32, (t, t), 0)
        r = (qi - kj).astype(F32) * slope2
        r_ref[0] = -r
        r_ref[1] = -jnp.abs(r)
        r_ref[2] = r

    qt = (q_ref[0].astype(F32) * (scale * LOG2E)).T
    row = lax.broadcasted_iota(I32, qt.shape, 0)
    qz_ref[0] = jnp.where(row < ATT_HALF_DIM, qt, 0.0).astype(BF16)
    qz_ref[1] = jnp.where(row >= ATT_HALF_DIM, qt, 0.0).astype(BF16)
    m_ref[...] = jnp.full(m_ref.shape, NEG_BIG, F32)
    acc_ref[...] = jnp.zeros(acc_ref.shape, F32)

    def tile_mode(jk):
        return jnp.where(jk < iq, 0, jnp.where(jk == iq, 1, 2))

    def scores(jk, s_ref):
        kt = k_ref[0, pl.ds(pl.multiple_of(jk * t, t), t), :]
        r = r_ref[tile_mode(jk)]
        for mp in (0, 1):
            s_ref[mp] = jnp.dot(kt, qz_ref[mp], preferred_element_type=F32) + r

    def update(jk, s_ref):
        c = -(slope2 * t) * jnp.abs(iq - jk).astype(F32)
        vt = vt_ref[jk]
        for mp in (0, 1):
            s = s_ref[mp]
            m_old = m_ref[mp]
            m_new = jnp.maximum(m_old, jnp.max(s, axis=0, keepdims=True) + c)
            a = jnp.exp2(m_old - m_new)
            p = jnp.exp2(s - (m_new - c))
            m_ref[mp] = m_new
            p_ref[...] = p.astype(BF16)
            acc_ref[mp] = a * acc_ref[mp] + jnp.dot(vt, p_ref[...], preferred_element_type=F32)

    scores(0, sa_ref)
    bufs = (sa_ref, sb_ref)
    for jk in range(nk):
        if jk + 1 < nk:
            scores(jk + 1, bufs[(jk + 1) % 2])
        update(jk, bufs[jk % 2])

    lv = lamv_ref[...]
    s1 = jnp.sum(lv[0:1] * lv[1:2], axis=-1, keepdims=True)
    s2 = jnp.sum(lv[2:3] * lv[3:4], axis=-1, keepdims=True)
    lam = jnp.exp(s1) - jnp.exp(s2) + lam_init
    vd = ATT_V_DIM
    o = (acc_ref[0, 0:vd, :] * (1.0 / acc_ref[0, vd:vd + 1, :])
         - lam * (acc_ref[1, 0:vd, :] * (1.0 / acc_ref[1, vd:vd + 1, :])))
    ms = jnp.mean(o * o, axis=0, keepdims=True)
    on = o * lax.rsqrt(ms + LN_EPS) * g_ref[...] * (1.0 - lam_init)
    o_ref[0] = on.T.astype(o_ref.dtype)


def _attention(proj3, lamv, subln_g, n_heads, lam_init, t):
    b, s, _ = proj3.shape
    nk = s // t
    vd = ATT_V_DIM
    slopes = jnp.asarray(2.0 ** (-8.0 * np.arange(1, n_heads + 1) / n_heads), dtype=F32)
    kern = functools.partial(_attn_kernel, t=t, nk=nk, lam_init=lam_init, scale=ATT_HALF_DIM ** -0.5)
    grid_spec = pltpu.PrefetchScalarGridSpec(
        num_scalar_prefetch=1, grid=(b, n_heads, nk),
        in_specs=[
            pl.BlockSpec((1, t, vd), lambda bi, hi, qi, sl: (bi, qi, hi)),
            pl.BlockSpec((1, s, vd), lambda bi, hi, qi, sl: (bi, 0, n_heads + hi)),
            pl.BlockSpec((1, s, vd), lambda bi, hi, qi, sl: (bi, 0, 2 * n_heads + hi)),
            pl.BlockSpec((4, ATT_HALF_DIM), lambda bi, hi, qi, sl: (0, 0)),
            pl.BlockSpec((vd, 1), lambda bi, hi, qi, sl: (0, 0)),
        ],
        out_specs=pl.BlockSpec((1, t, vd), lambda bi, hi, qi, sl: (bi, qi, hi)),
        scratch_shapes=[
            pltpu.VMEM((nk, vd + 16, t), BF16),
            pltpu.VMEM((2, vd, t), BF16),
            pltpu.VMEM((3, t, t), F32),
            pltpu.VMEM((2, t, t), F32),
            pltpu.VMEM((2, t, t), F32),
            pltpu.VMEM((t, t), BF16),
            pltpu.VMEM((2, 1, t), F32),
            pltpu.VMEM((2, vd + 16, t), F32),
        ],
    )
    return pl.pallas_call(
        kern, grid_spec=grid_spec,
        out_shape=jax.ShapeDtypeStruct((b, s, n_heads * vd), BF16),
        compiler_params=_cparams(("parallel", "parallel", "arbitrary"), 48), name="diff_attn",
    )(slopes, proj3, proj3, proj3, lamv, subln_g.reshape(vd, 1))


def _conv_kernel(vc_ref, gc_ref, vp_ref, gp_ref, vn_ref, gn_ref, w_ref, b_ref, lg_ref, lb_ref,
                 o_ref, ext_ref, sh_ref, *, ts, width, rc):
    i = pl.program_id(1)
    n = pl.num_programs(1)

    def glu(v_ref, g_ref):
        return v_ref[0].astype(F32) * jax.nn.sigmoid(g_ref[0].astype(F32))

    hp = CONV_HALO
    ext_ref[0:hp, :] = glu(vp_ref, gp_ref) * (i > 0).astype(F32)
    ext_ref[hp:hp + ts, :] = glu(vc_ref, gc_ref)
    ext_ref[hp + ts:hp + ts + hp, :] = glu(vn_ref, gn_ref) * (i < n - 1).astype(F32)
    rows = sh_ref.shape[1]
    for s in range(8):
        sh_ref[s] = ext_ref[s:s + rows, :]
    pad = width // 2
    c = o_ref.shape[-1]
    for r0 in range(0, ts, rc):
        acc = jnp.zeros((rc, c), F32) + b_ref[...]
        for j in range(width):
            off = r0 + hp - pad + j
            acc = acc + sh_ref[off % 8, off - off % 8:off - off % 8 + rc, :] * w_ref[j:j + 1, :]
        y = _ln_rows(acc, lg_ref[...], lb_ref[...])
        o_ref[0, r0:r0 + rc, :] = (y * jax.nn.sigmoid(y)).astype(o_ref.dtype)


def _conv_branch(proj3, w_dw, b_dw, g_ln, b_ln, col_val, ts):
    b, s, _ = proj3.shape
    width, c = w_dw.shape
    assert width // 2 <= CONV_HALO and col_val % c == 0 and ts % CONV_HALO == 0
    iv = col_val // c
    hb = ts // CONV_HALO
    nhb = s // CONV_HALO
    cur = lambda off: pl.BlockSpec((1, ts, c), lambda bi, i: (bi, i, iv + off))
    prev = lambda off: pl.BlockSpec((1, CONV_HALO, c), lambda bi, i: (bi, jnp.maximum(i * hb - 1, 0), iv + off))
    nxt = lambda off: pl.BlockSpec((1, CONV_HALO, c), lambda bi, i: (bi, jnp.minimum((i + 1) * hb, nhb - 1), iv + off))
    vec = pl.BlockSpec((1, c), lambda bi, i: (0, 0))
    kern = functools.partial(_conv_kernel, ts=ts, width=width, rc=32)
    return pl.pallas_call(
        kern, grid=(b, s // ts),
        in_specs=[cur(0), cur(1), prev(0), prev(1), nxt(0), nxt(1),
                  pl.BlockSpec((width, c), lambda bi, i: (0, 0)), vec, vec, vec],
        out_specs=pl.BlockSpec((1, ts, c), lambda bi, i: (bi, i, 0)),
        out_shape=jax.ShapeDtypeStruct((b, s, c), BF16),
        scratch_shapes=[pltpu.VMEM((ts + 2 * CONV_HALO, c), F32), pltpu.VMEM((8, ts + CONV_HALO + 8, c), F32)],
        compiler_params=_cparams(("parallel", "parallel"), 48), name="conv_branch",
    )(proj3, proj3, proj3, proj3, proj3, proj3, w_dw, b_dw.reshape(1, c), g_ln.reshape(1, c), b_ln.reshape(1, c))


def _merge_kernel(o_ref, c_ref, wpa_ref, wpc_ref, bpc_ref, ga_ref, gc_ref, bga_ref, bgc_ref, out_ref):
    ya = jnp.dot(o_ref[...], wpa_ref[...], preferred_element_type=F32)
    yc = jnp.dot(c_ref[...], wpc_ref[...], preferred_element_type=F32) + bpc_ref[...]
    g_att = jax.nn.sigmoid(ga_ref[...].astype(F32) + bga_ref[...])
    g_conv = jax.nn.sigmoid(gc_ref[...].astype(F32) + bgc_ref[...])
    out_ref[...] = (g_att * ya + g_conv * yc).astype(out_ref.dtype)


def _merge(o2, c2, w_pa, w_pc, b_pc, proj2, b_gate, col_gate, tm, tn):
    n, da = o2.shape
    dc = c2.shape[1]
    d = w_pa.shape[1]
    assert col_gate % tn == 0 and d % tn == 0
    ig = col_gate // tn
    nj = d // tn
    vec = lambda off: pl.BlockSpec((1, tn), lambda i, j: (0, off + j))
    return pl.pallas_call(
        _merge_kernel, grid=(n // tm, nj),
        in_specs=[pl.BlockSpec((tm, da), lambda i, j: (i, 0)), pl.BlockSpec((tm, dc), lambda i, j: (i, 0)),
                  pl.BlockSpec((da, tn), lambda i, j: (0, j)), pl.BlockSpec((dc, tn), lambda i, j: (0, j)), vec(0),
                  pl.BlockSpec((tm, tn), lambda i, j: (i, ig + j)),
                  pl.BlockSpec((tm, tn), lambda i, j: (i, ig + nj + j)),
                  vec(0), vec(nj)],
        out_specs=pl.BlockSpec((tm, tn), lambda i, j: (i, j)),
        out_shape=jax.ShapeDtypeStruct((n, d), BF16),
        compiler_params=_cparams(("parallel", "parallel"), 48), name="merge",
    )(o2, c2, w_pa, w_pc, b_pc.reshape(1, d), proj2, proj2, b_gate.reshape(1, 2 * d), b_gate.reshape(1, 2 * d))


def _outproj_ln_kernel(m_ref, w_ref, b_ref, h_ref, g_ref, be_ref, o_ref, *, alpha):
    mix = jnp.dot(m_ref[...], w_ref[...], preferred_element_type=F32) + b_ref[...]
    o_ref[...] = _ln_rows(alpha * h_ref[...] + mix, g_ref[...], be_ref[...])


def _outproj_ln(merged, w_out, b_out, h, g, b, alpha, tm):
    n, d = h.shape
    row = pl.BlockSpec((tm, d), lambda i: (i, 0))
    vec = pl.BlockSpec((1, d), lambda i: (0, 0))
    return pl.pallas_call(
        functools.partial(_outproj_ln_kernel, alpha=alpha), grid=(n // tm,),
        in_specs=[row, pl.BlockSpec((d, d), lambda i: (0, 0)), vec, row, vec, vec],
        out_specs=row, out_shape=jax.ShapeDtypeStruct((n, d), F32),
        compiler_params=_cparams(("parallel",), 56), name="outproj_ln",
    )(merged, w_out, b_out.reshape(1, d), h, g.reshape(1, d), b.reshape(1, d))


def _router_kernel(h_ref, wr_ref, br_ref, idx_ref, gate_ref, rank_ref, cnt_ref, carry_ref, *, n_exp, t):
    i = pl.program_id(0)

    @pl.when(i == 0)
    def _():
        carry_ref[...] = jnp.zeros(carry_ref.shape, F32)

    logits = lax.dot_general(wr_ref[...], h_ref[...], (((1,), (1,)), ((), ())),
                             precision=lax.Precision.HIGHEST, preferred_element_type=F32) + br_ref[...]
    iota_e = lax.broadcasted_iota(I32, (n_exp, t), 0).astype(F32)
    work = logits
    member = jnp.zeros((n_exp, t), F32)
    vals, idxs = [], []
    for _ in range(TOP_K):
        mx = jnp.max(work, axis=0, keepdims=True)
        idx = jnp.min(jnp.where(work == mx, iota_e, float(n_exp)), axis=0, keepdims=True)
        sel = iota_e == idx
        vals.append(mx)
        idxs.append(idx)
        member = member + sel.astype(F32)
        work = jnp.where(sel, -jnp.inf, work)
    es = [jnp.exp(v - vals[0]) for v in vals]
    inv = 1.0 / functools.reduce(lambda a, b: a + b, es)
    earlier = (lax.broadcasted_iota(I32, (t, t), 0) < lax.broadcasted_iota(I32, (t, t), 1)).astype(BF16)
    carry = carry_ref[...]
    excl = jnp.dot(member.astype(BF16), earlier, preferred_element_type=F32) + carry
    for k in range(TOP_K):
        idx_ref[k:k + 1, :] = idxs[k].astype(I32)
        gate_ref[k:k + 1, :] = es[k] * inv
        rank_ref[k:k + 1, :] = jnp.sum(jnp.where(iota_e == idxs[k], excl, 0.0), axis=0, keepdims=True).astype(I32)
    carry_new = carry + jnp.sum(member, axis=1, keepdims=True)
    carry_ref[...] = carry_new
    cnt_ref[...] = jnp.broadcast_to(carry_new, cnt_ref.shape).astype(I32)


def _router(h, w_router, b_router, t):
    n, d = h.shape
    e = w_router.shape[1]
    out_tok = pl.BlockSpec((TOP_K, t), lambda i: (0, i))
    return pl.pallas_call(
        functools.partial(_router_kernel, n_exp=e, t=t), grid=(n // t,),
        in_specs=[pl.BlockSpec((t, d), lambda i: (i, 0)), pl.BlockSpec((e, d), lambda i: (0, 0)),
                  pl.BlockSpec((e, 1), lambda i: (0, 0))],
        out_specs=[out_tok, out_tok, out_tok, pl.BlockSpec((e, 128), lambda i: (0, 0))],
        out_shape=[jax.ShapeDtypeStruct((TOP_K, n), I32), jax.ShapeDtypeStruct((TOP_K, n), F32),
                   jax.ShapeDtypeStruct((TOP_K, n), I32), jax.ShapeDtypeStruct((e, 128), I32)],
        scratch_shapes=[pltpu.VMEM((e, 1), F32)],
        compiler_params=_cparams(("arbitrary",), 48), name="router",
    )(h, w_router.T, b_router.reshape(e, 1))


def _plan_kernel(cnt_ref, idx_ref, rank_ref, pos_ref, be_ref, nu_ref, pad_ref, *, n_exp, tm_log2, tn, nbp):
    tm = 1 << tm_log2
    cnt = cnt_ref[:, 0:n_exp]
    padded = (((cnt + (tm - 1)) >> tm_log2) << tm_log2).astype(F32)
    row = lax.broadcasted_iota(I32, (n_exp, n_exp), 0)
    col = lax.broadcasted_iota(I32, (n_exp, n_exp), 1)
    padded_row = jnp.sum(jnp.where(row == col, padded, 0.0), axis=0, keepdims=True)
    pstart = jnp.sum(jnp.where(col < row, padded_row, 0.0), axis=1, keepdims=True)
    pend = pstart + padded[:, 0:1]
    iota_e = lax.broadcasted_iota(I32, (n_exp, tn), 0)
    for k in range(TOP_K):
        hit = iota_e == idx_ref[k:k + 1, :]
        pos_ref[k:k + 1, :] = (jnp.sum(jnp.where(hit, pstart, 0.0), axis=0, keepdims=True).astype(I32)
                               + rank_ref[k:k + 1, :])
    blk_start = (lax.broadcasted_iota(I32, (n_exp, nbp), 1) * tm).astype(F32)
    be = jnp.sum((pend <= blk_start).astype(F32), axis=0, keepdims=True)
    be_ref[...] = jnp.minimum(be, n_exp - 1.0).astype(I32)
    total = jnp.max(pend, axis=0, keepdims=True)
    nu_ref[...] = jnp.broadcast_to(total, nu_ref.shape).astype(I32) >> tm_log2
    cnt_w = cnt_ref[...]
    padded_w = (((cnt_w + (tm - 1)) >> tm_log2) << tm_log2).astype(F32)
    row_w = lax.broadcasted_iota(I32, cnt_w.shape, 0)
    col_w = lax.broadcasted_iota(I32, cnt_w.shape, 1)
    on_diag = row_w == col_w
    pstart_row = jnp.sum(jnp.where(row_w < col_w, padded_w, 0.0), axis=0, keepdims=True)
    cnt_row = jnp.sum(jnp.where(on_diag, cnt_w.astype(F32), 0.0), axis=0, keepdims=True)
    padded_row_w = jnp.sum(jnp.where(on_diag, padded_w, 0.0), axis=0, keepdims=True)
    pad_ref[0:1, :] = (pstart_row + cnt_row).astype(I32)
    pad_ref[1:2, :] = (pstart_row + padded_row_w).astype(I32)


def _plan(cnt, idx, rank, tm, nb, tn):
    e = cnt.shape[0]
    n = idx.shape[1]
    nbp = ((nb + 127) // 128) * 128
    tm_log2 = int(math.log2(tm))
    assert 1 << tm_log2 == tm
    tok = pl.BlockSpec((TOP_K, tn), lambda i: (0, i))
    return pl.pallas_call(
        functools.partial(_plan_kernel, n_exp=e, tm_log2=tm_log2, tn=tn, nbp=nbp), grid=(n // tn,),
        in_specs=[pl.BlockSpec((e, 128), lambda i: (0, 0)), tok, tok],
        out_specs=[tok, pl.BlockSpec((1, nbp), lambda i: (0, 0)), pl.BlockSpec((1, 128), lambda i: (0, 0)),
                   pl.BlockSpec((2, 128), lambda i: (0, 0))],
        out_shape=[jax.ShapeDtypeStruct((TOP_K, n), I32), jax.ShapeDtypeStruct((1, nbp), I32),
                   jax.ShapeDtypeStruct((1, 128), I32), jax.ShapeDtypeStruct((2, 128), I32)],
        compiler_params=_cparams(("arbitrary",), 48), name="plan",
    )(cnt, idx, rank)


def _dispatch_kernel(pos_ref, pad_ref, nu_ref, h_ref, xs_ref, zero_ref, sem, zsem, *, t, n, n_exp, tm, nb):
    base = pl.program_id(0) * t

    @pl.when(pl.program_id(0) == 0)
    def _():
        zero_ref[...] = jnp.zeros(zero_ref.shape, zero_ref.dtype)

        def zero_copy(r):
            return pltpu.make_async_copy(zero_ref.at[pl.ds(0, 1)], xs_ref.at[pl.ds(r, 1)], zsem)

        def zero_block(blk):
            return pltpu.make_async_copy(zero_ref, xs_ref.at[pl.ds(pl.multiple_of(blk * tm, tm), tm)], zsem)

        lax.fori_loop(nu_ref[0], nb, lambda blk, c: (zero_block(blk).start(), c)[1], 0)
        lax.fori_loop(nu_ref[0], nb, lambda blk, c: (zero_block(blk).wait(), c)[1], 0)

        def zero_rows8(r8):
            return pltpu.make_async_copy(zero_ref.at[pl.ds(0, 8)], xs_ref.at[pl.ds(pl.multiple_of(r8 * 8, 8), 8)], zsem)

        for phase in ("start", "wait"):
            for e in range(n_exp):
                lo, hi = pad_ref[e], pad_ref[128 + e]
                mid = jnp.minimum(((lo + 7) >> 3) << 3, hi)
                lax.fori_loop(lo, mid, lambda r, c: (getattr(zero_copy(r), phase)(), c)[1], 0)
                lax.fori_loop(mid >> 3, hi >> 3, lambda r8, c: (getattr(zero_rows8(r8), phase)(), c)[1], 0)

    for k in range(TOP_K):
        def body(j, carry, k=k):
            r = pos_ref[k * n + base + j]
            pltpu.make_async_copy(h_ref.at[pl.ds(j, 1)], xs_ref.at[pl.ds(r, 1)], sem).start()
            return carry
        lax.fori_loop(0, t, body, 0, unroll=16)
    for k in range(TOP_K):
        pltpu.make_async_copy(h_ref, xs_ref.at[pl.ds(0, t)], sem).wait()


def _dispatch(pos_flat, pad_flat, nu, h, n_rows, n_exp, tm, t):
    n, d = h.shape
    grid_spec = pltpu.PrefetchScalarGridSpec(
        num_scalar_prefetch=3, grid=(n // t,),
        in_specs=[pl.BlockSpec((t, d), lambda i, p, q, u: (i, 0))],
        out_specs=pl.BlockSpec(memory_space=pl.ANY),
        scratch_shapes=[pltpu.VMEM((tm, d), h.dtype), pltpu.SemaphoreType.DMA(()), pltpu.SemaphoreType.DMA(())],
    )
    return pl.pallas_call(
        functools.partial(_dispatch_kernel, t=t, n=n, n_exp=n_exp, tm=tm, nb=n_rows // tm), grid_spec=grid_spec,
        out_shape=jax.ShapeDtypeStruct((n_rows, d), h.dtype),
        compiler_params=_cparams(("arbitrary",), 48), name="dispatch",
    )(pos_flat, pad_flat, nu, h)


def _used_block(i, nu_ref):
    return jnp.minimum(i, nu_ref[0] - 1)


def _expert_changed(i, be_ref, nu_ref):
    cur = be_ref[_used_block(i, nu_ref)]
    prev = be_ref[_used_block(jnp.maximum(i - 1, 0), nu_ref)]
    return jnp.logical_or(i == 0, cur != prev)


def _next_run_start(i, e, be_ref, nu_ref):
    def same_expert(i2):
        return jnp.logical_and(i2 < nu_ref[0], be_ref[_used_block(i2, nu_ref)] == e)
    return lax.while_loop(same_expert, lambda i2: i2 + 1, i + 1)


def _gu_kernel(be_ref, nu_ref, x_ref, w_hbm, bg_ref, bu_ref, o_ref, sg_ref, su_ref, wgb_ref, wub_ref, sem,
               *, layer, f, tn):
    j = pl.program_id(0)
    i = pl.program_id(1)
    nj = pl.num_programs(0)

    def copies(e, jj):
        cg = pltpu.make_async_copy(w_hbm.at[layer, e, :, pl.ds(pl.multiple_of(jj * tn, tn), tn)], sg_ref, sem.at[0])
        cu = pltpu.make_async_copy(w_hbm.at[layer, e, :, pl.ds(pl.multiple_of(f + jj * tn, tn), tn)], su_ref,
                                   sem.at[1])
        return cg, cu

    @pl.when(_expert_changed(i, be_ref, nu_ref))
    def _():
        e = be_ref[_used_block(i, nu_ref)]

        @pl.when(jnp.logical_and(j == 0, i == 0))
        def _():
            for cp in copies(e, j):
                cp.start()

        for cp in copies(e, j):
            cp.wait()
        wgb_ref[...] = sg_ref[...].astype(BF16)
        wub_ref[...] = su_ref[...].astype(BF16)
        i2 = _next_run_start(i, e, be_ref, nu_ref)
        more = i2 < nu_ref[0]
        e2 = jnp.where(more, be_ref[_used_block(i2, nu_ref)], be_ref[0])
        j2 = jnp.where(more, j, j + 1)

        @pl.when(j2 < nj)
        def _():
            for cp in copies(e2, j2):
                cp.start()

    @pl.when(i < nu_ref[0])
    def _():
        x = x_ref[...].astype(BF16)
        g = jnp.dot(x, wgb_ref[...], preferred_element_type=F32) + bg_ref[...]
        u = jnp.dot(x, wub_ref[...], preferred_element_type=F32) + bu_ref[...]
        g = jnp.minimum(g, SWIGLU_LIMIT)
        u = jnp.clip(u, -SWIGLU_LIMIT, SWIGLU_LIMIT)
        o_ref[...] = (g * jax.nn.sigmoid(SWIGLU_ALPHA * g) * (u + 1.0)).astype(o_ref.dtype)

    @pl.when(i >= nu_ref[0])
    def _():
        o_ref[...] = jnp.zeros(o_ref.shape, o_ref.dtype)


def _expert_gate_up(be, nu, x, w_gu, b_gu, layer, tm, tn):
    n_rows, kd = x.shape
    nl, e, _, f2 = w_gu.shape
    f = f2 // 2
    nj = f // tn
    row = lambda j, i, be_r, nu_r: (_used_block(i, nu_r), 0)
    bmap = lambda off: (lambda j, i, be_r, nu_r: (layer, be_r[_used_block(i, nu_r)], 0, off + j))
    grid_spec = pltpu.PrefetchScalarGridSpec(
        num_scalar_prefetch=2, grid=(nj, n_rows // tm),
        in_specs=[pl.BlockSpec((tm, kd), row), pl.BlockSpec(memory_space=pl.ANY),
                  pl.BlockSpec((None, None, 1, tn), bmap(0)), pl.BlockSpec((None, None, 1, tn), bmap(nj))],
        out_specs=pl.BlockSpec((tm, tn), lambda j, i, be_r, nu_r: (i, j)),
        scratch_shapes=[pltpu.VMEM((kd, tn), F32), pltpu.VMEM((kd, tn), F32),
                        pltpu.VMEM((kd, tn), BF16), pltpu.VMEM((kd, tn), BF16), pltpu.SemaphoreType.DMA((2,))],
    )
    b4 = b_gu.reshape(nl, e, 1, f2)
    return pl.pallas_call(
        functools.partial(_gu_kernel, layer=layer, f=f, tn=tn), grid_spec=grid_spec,
        out_shape=jax.ShapeDtypeStruct((n_rows, f), BF16),
        compiler_params=_cparams(("arbitrary", "arbitrary"), 48), name="expert_gate_up",
    )(be, nu, x, w_gu, b4, b4)


def _dn_kernel(be_ref, nu_ref, x_ref, w_hbm, b_ref, o_ref, sw_ref, wb_ref, sem, *, layer):
    i = pl.program_id(0)

    def copy(e):
        return pltpu.make_async_copy(w_hbm.at[layer, e], sw_ref, sem)

    @pl.when(_expert_changed(i, be_ref, nu_ref))
    def _():
        e = be_ref[_used_block(i, nu_ref)]

        @pl.when(i == 0)
        def _():
            copy(e).start()

        copy(e).wait()
        wb_ref[...] = sw_ref[...].astype(BF16)
        i2 = _next_run_start(i, e, be_ref, nu_ref)

        @pl.when(i2 < nu_ref[0])
        def _():
            copy(be_ref[_used_block(i2, nu_ref)]).start()

    @pl.when(i < nu_ref[0])
    def _():
        o_ref[...] = jnp.dot(x_ref[...], wb_ref[...], preferred_element_type=F32) + b_ref[...]

    @pl.when(i >= nu_ref[0])
    def _():
        o_ref[...] = jnp.zeros(o_ref.shape, o_ref.dtype)


def _expert_down(be, nu, hid, w_dn, b_dn, layer, tm):
    n_rows, f = hid.shape
    nl, e, _, d = w_dn.shape
    bmap = lambda i, be_r, nu_r: (layer, be_r[_used_block(i, nu_r)], 0, 0)
    grid_spec = pltpu.PrefetchScalarGridSpec(
        num_scalar_prefetch=2, grid=(n_rows // tm,),
        in_specs=[pl.BlockSpec((tm, f), lambda i, be_r, nu_r: (_used_block(i, nu_r), 0)),
                  pl.BlockSpec(memory_space=pl.ANY), pl.BlockSpec((None, None, 1, d), bmap)],
        out_specs=pl.BlockSpec((tm, d), lambda i, be_r, nu_r: (i, 0)),
        scratch_shapes=[pltpu.VMEM((f, d), F32), pltpu.VMEM((f, d), BF16), pltpu.SemaphoreType.DMA(())],
    )
    return pl.pallas_call(
        functools.partial(_dn_kernel, layer=layer), grid_spec=grid_spec,
        out_shape=jax.ShapeDtypeStruct((n_rows, d), F32),
        compiler_params=_cparams(("arbitrary",), 48), name="expert_down",
    )(be, nu, hid, w_dn, b_dn.reshape(nl, e, 1, d))


def _combine_kernel(pos_ref, y_ref, h_ref, gt_ref, g_ref, b_ref, o_ref, ob_ref, ybuf, sem, *, t, n, alpha):
    i = pl.program_id(0)
    nsteps = pl.num_programs(0)

    def issue(step, slot):
        base = step * t
        for k in range(TOP_K):
            def body(j, carry, k=k):
                r = pos_ref[k * n + base + j]
                pltpu.make_async_copy(y_ref.at[pl.ds(r, 1)], ybuf.at[slot, k, pl.ds(j, 1)], sem.at[slot]).start()
                return carry
            lax.fori_loop(0, t, body, 0, unroll=16)

    @pl.when(i == 0)
    def _():
        issue(0, 0)

    slot = i % 2

    @pl.when(i + 1 < nsteps)
    def _():
        issue(i + 1, 1 - slot)

    for k in range(TOP_K):
        pltpu.make_async_copy(y_ref.at[pl.ds(0, t)], ybuf.at[slot, k], sem.at[slot]).wait()
    gt = gt_ref[...]
    ff = gt[:, 0:1] * ybuf[slot, 0]
    for k in range(1, TOP_K):
        ff = ff + gt[:, k:k + 1] * ybuf[slot, k]
    out = _ln_rows(alpha * h_ref[...] + ff, g_ref[...], b_ref[...])
    o_ref[...] = out
    ob_ref[...] = out.astype(BF16)


def _combine(pos_flat, y, h, gates_t, g, b, alpha, t):
    n, d = h.shape
    row = pl.BlockSpec((t, d), lambda i, p: (i, 0))
    vec = pl.BlockSpec((1, d), lambda i, p: (0, 0))
    grid_spec = pltpu.PrefetchScalarGridSpec(
        num_scalar_prefetch=1, grid=(n // t,),
        in_specs=[pl.BlockSpec(memory_space=pl.ANY), row, pl.BlockSpec((t, TOP_K), lambda i, p: (i, 0)), vec, vec],
        out_specs=[row, row],
        scratch_shapes=[pltpu.VMEM((2, TOP_K, t, d), F32), pltpu.SemaphoreType.DMA((2,))],
    )
    return pl.pallas_call(
        functools.partial(_combine_kernel, t=t, n=n, alpha=alpha), grid_spec=grid_spec,
        out_shape=[jax.ShapeDtypeStruct((n, d), F32), jax.ShapeDtypeStruct((n, d), BF16)],
        compiler_params=_cparams(("arbitrary",), 48), name="combine",
    )(pos_flat, y, h, gates_t, g.reshape(1, d), b.reshape(1, d))


def kernel(x, ln_in_g, ln_in_b, w_in, b_gate, lambda_q1, lambda_k1, lambda_q2, lambda_k2, subln_g, w_dw, b_dw,
           conv_ln_g, conv_ln_b, w_pa, w_pc, b_pc, w_out, b_out, ln1_g, ln1_b, w_router, b_router, w_gu, b_gu,
           w_dn, b_dn, ln2_g, ln2_b):
    bsz, seq, d = x.shape
    depth = w_in.shape[0]
    d_att = w_pa.shape[1]
    d_conv = w_pc.shape[1]
    n_heads = d_att // ATT_V_DIM
    n_exp = w_router.shape[-1]
    d_ff = w_dn.shape[-2]
    n = bsz * seq
    alpha = (2 * depth) ** 0.25

    tm_rows = min(512, n)
    t_att = min(512, seq)
    ts_conv = min(512, seq)
    tm_moe = min(512, n)
    t_router = min(512, n)
    t_disp = min(512, n)
    t_comb = min(256, n)
    n_rows = n * TOP_K + n_exp * tm_moe
    nb = n_rows // tm_moe

    h, hb = _layer_norm_in(x.reshape(n, d), ln_in_g, ln_in_b, tm_rows)
    for l in range(depth):
        lam_init = 0.8 - 0.6 * math.exp(-0.3 * l)
        w_in_b = w_in[l].astype(BF16)
        d_in = w_in_b.shape[1]
        proj = _matmul(hb, w_in_b, _tile(n, 1024), _tile(d_in, 1536), BF16, "in_proj")
        proj3 = proj.reshape(bsz, seq, d_in)
        lamv = jnp.stack([lambda_q1[l], lambda_k1[l], lambda_q2[l], lambda_k2[l]]).astype(F32)
        o = _attention(proj3, lamv, subln_g[l], n_heads, lam_init, t_att)
        c = _conv_branch(proj3, w_dw[l].reshape(w_dw.shape[1], d_conv), b_dw[l], conv_ln_g[l], conv_ln_b[l],
                         3 * d_att, ts_conv)
        merged = _merge(o.reshape(n, d_att), c.reshape(n, d_conv), w_pa[l].astype(BF16), w_pc[l].astype(BF16),
                        b_pc[l], proj, b_gate[l], 3 * d_att + 2 * d_conv, _tile(n, 1024), min(1024, d))
        h1 = _outproj_ln(merged, w_out[l].astype(BF16), b_out[l], h, ln1_g[l], ln1_b[l], alpha, min(512, n))

        idx, gates, rank, cnt = _router(h1, w_router[l], b_router[l], t_router)
        pos, be, nu, pad = _plan(cnt, idx, rank, tm_moe, nb, min(2048, n))
        pos_flat = pos.reshape(TOP_K * n)
        be_flat = be.reshape(-1)
        nu_flat = nu.reshape(-1)[:1]
        xs = _dispatch(pos_flat, pad.reshape(-1), nu_flat, h1, n_rows, n_exp, tm_moe, t_disp)
        hid = _expert_gate_up(be_flat, nu_flat, xs, w_gu, b_gu, l, tm_moe, _tile(d_ff, 1024))
        y = _expert_down(be_flat, nu_flat, hid, w_dn, b_dn, l, tm_moe)
        h, hb = _combine(pos_flat, y, h1, gates.T, ln2_g[l], ln2_b[l], alpha, t_comb)
    return h.reshape(bsz, seq, d)
```
